```python
import math
import jax
import jax.numpy as jnp
from jax import lax
import numpy as np

D_MODEL = 2048
BATCH = 4
SEQ = 2048
DEPTH = 4
DEC_BATCH = 128
DEC_SEQ = 8
PAST_LEN = 8192
PAGE_SIZE = 128

N_META = 16
N_MIXERS = 3
N_A_LAYERS = (DEPTH + N_MIXERS - 1) // N_MIXERS
N_B_LAYERS = (DEPTH + N_MIXERS - 2) // N_MIXERS
N_C_LAYERS = DEPTH // N_MIXERS

A_HEADS = 16
A_HEAD_DIM = D_MODEL // A_HEADS
A_KV_HEADS = 2
IDX_HEADS = 16
IDX_DIM = 64
TOPK_MAX = 256
N_BUCKETS = 32
MAX_DISTANCE = 128
B_HEADS = 16
NOPE_DIM = 128
ROPE_DIM = 64
V_DIM = 128
KV_LORA = 512
Q_LORA = 512
ROPE_THETA = 10000.0
CONV_W = 3
D_FF = ((8 * D_MODEL // 3 + 255) // 256) * 256

Q_BLOCK = 128
EPS = 1e-6
NEG = -1e30
A_SCALE = A_HEAD_DIM ** -0.5
IDX_SCALE = (IDX_DIM * IDX_HEADS) ** -0.5
B_SCALE = (NOPE_DIM + ROPE_DIM) ** -0.5

kernel_name = 'hybrid_dsa_mla_shortconv_decode_step'


def rmsnorm(x, g):
    x32 = x.astype(jnp.float32)
    y = x32 * lax.rsqrt(jnp.mean(x32 * x32, axis=-1, keepdims=True) + EPS)
    return y.astype(x.dtype) * g


def rope(x, pos):
    half = ROPE_DIM // 2
    inv = jnp.power(ROPE_THETA, -jnp.arange(half, dtype=jnp.float32) / half)
    ang = pos.astype(jnp.float32)[..., None] * inv
    cos, sin = jnp.cos(ang), jnp.sin(ang)
    x32 = x.astype(jnp.float32)
    x1, x2 = x32[..., :half], x32[..., half:]
    return jnp.concatenate([x1 * cos - x2 * sin, x2 * cos + x1 * sin], axis=-1).astype(x.dtype)


def rel_bucket(dist):
    n = jnp.maximum(dist, 0)
    max_exact = N_BUCKETS // 2
    nf = jnp.maximum(n, 1).astype(jnp.float32)
    large = max_exact + (jnp.log(nf / max_exact) / math.log(MAX_DISTANCE / max_exact)
                         * (N_BUCKETS - max_exact)).astype(jnp.int32)
    large = jnp.minimum(large, N_BUCKETS - 1)
    return jnp.where(n < max_exact, n, large)


def take_rows(a, idx):
    return jax.vmap(lambda ab, ib: ab[ib])(a, idx)


def sweep_queries(fn, q_arrays, t):
    head = fn(*[a[:, :N_META] for a in q_arrays], jnp.arange(N_META, dtype=jnp.int32))
    nb = (t - N_META) // Q_BLOCK
    blocks = tuple(jnp.moveaxis(a[:, N_META:].reshape((a.shape[0], nb, Q_BLOCK) + a.shape[2:]), 1, 0)
                   for a in q_arrays)
    pos = (N_META + jnp.arange(t - N_META, dtype=jnp.int32)).reshape(nb, Q_BLOCK)
    body = lax.map(lambda args: fn(*args), blocks + (pos,))
    body = jnp.moveaxis(body, 0, 1)
    body = body.reshape((body.shape[0], nb * Q_BLOCK) + body.shape[3:])
    return jnp.concatenate([head, body], axis=1)


def dsa_project(h, w):
    wq, wk, wv, q_norm, k_norm, w_iq, w_ik, w_iw = w
    b, t, _ = h.shape
    q = rmsnorm((h @ wq).reshape(b, t, A_HEADS, A_HEAD_DIM), q_norm)
    k = rmsnorm((h @ wk).reshape(b, t, A_KV_HEADS, A_HEAD_DIM), k_norm)
    v = (h @ wv).reshape(b, t, A_KV_HEADS, A_HEAD_DIM)
    qi = (h @ w_iq).reshape(b, t, IDX_HEADS, IDX_DIM)
    ki = h @ w_ik
    wi = h @ w_iw
    return q, k, v, qi, ki, wi


def dsa_attend(q, qi, wi, q_pos, k_idx, gather_kv, topk, rel_bias):
    b, tq = q.shape[0], q.shape[1]
    n_rep = A_HEADS // A_KV_HEADS
    key_pos = jnp.arange(k_idx.shape[1], dtype=jnp.int32)
    dots = jnp.einsum('bqhd,bkd->bqhk', qi, k_idx).astype(jnp.float32)
    score = jnp.einsum('bqhk,bqh->bqk', jax.nn.relu(dots), wi.astype(jnp.float32)) * IDX_SCALE
    score = jnp.where((key_pos[None, :] <= q_pos[:, None])[None], score, NEG)
    _, sel = lax.top_k(score, topk)
    valid = sel <= q_pos[None, :, None]
    k_sel, v_sel = gather_kv(sel)
    qg = q.reshape(b, tq, A_KV_HEADS, n_rep, A_HEAD_DIM)
    logits = jnp.einsum('bqgrd,bqkgd->bqgrk', qg, k_sel).astype(jnp.float32) * A_SCALE
    bias = rel_bias[rel_bucket(q_pos[None, :, None] - sel)]
    bias = bias.reshape(b, tq, topk, A_KV_HEADS, n_rep).transpose(0, 1, 3, 4, 2)
    logits = jnp.where(valid[:, :, None, None, :], logits + bias.astype(jnp.float32), NEG)
    p = jax.nn.softmax(logits, axis=-1).astype(v_sel.dtype)
    o = jnp.einsum('bqgrk,bqkgd->bqgrd', p, v_sel)
    return o.reshape(b, tq, A_HEADS * A_HEAD_DIM)


def dsa_prompt(h, rel_bias, w):
    q, k, v, qi, ki, wi = dsa_project(h, w)
    t = h.shape[1]
    topk = min(TOPK_MAX, t // 4)

    def gather_kv(sel):
        return take_rows(k, sel), take_rows(v, sel)

    def fn(q_b, qi_b, wi_b, pos_b):
        return dsa_attend(q_b, qi_b, wi_b, pos_b, ki, gather_kv, topk, rel_bias)

    return sweep_queries(fn, (q, qi, wi), t), (k, v, ki)


def dsa_sample(h, q_pos, rel_bias, w, cache_k, cache_v, cache_ik, layer, page_table):
    q, k, v, qi, ki, wi = dsa_project(h, w)
    b, tn = h.shape[0], h.shape[1]
    past_ki = cache_ik[layer, page_table].reshape(b, PAST_LEN, IDX_DIM).astype(ki.dtype)
    k_idx = jnp.concatenate([past_ki, ki], axis=1)
    topk = min(TOPK_MAX, (PAST_LEN + tn) // 4)

    def gather_kv(sel):
        in_past = (sel < PAST_LEN)[..., None, None]
        sp = jnp.minimum(sel, PAST_LEN - 1)
        phys = jnp.take_along_axis(page_table, (sp // PAGE_SIZE).reshape(b, -1), axis=1).reshape(sel.shape)
        off = sp % PAGE_SIZE
        sn = jnp.clip(sel - PAST_LEN, 0, tn - 1)
        k_sel = jnp.where(in_past, cache_k[layer, phys, off].astype(k.dtype), take_rows(k, sn))
        v_sel = jnp.where(in_past, cache_v[layer, phys, off].astype(v.dtype), take_rows(v, sn))
        return k_sel, v_sel

    o = dsa_attend(q, qi, wi, q_pos, k_idx, gather_kv, topk, rel_bias)
    return o, (k, v, ki)


def mla_project(h, pos, w):
    w_dq, q_a_norm, w_uq, q_norm, w_dkv, kv_norm, kpe_norm, w_uk = w[:8]
    b, t, _ = h.shape
    q = (rmsnorm(h @ w_dq, q_a_norm) @ w_uq).reshape(b, t, B_HEADS, NOPE_DIM + ROPE_DIM)
    q = rmsnorm(q, q_norm)
    q_lat = jnp.einsum('bthn,hnc->bthc', q[..., :NOPE_DIM], w_uk)
    q_pe = rope(q[..., NOPE_DIM:], pos[:, None])
    kv = h @ w_dkv
    ckv = rmsnorm(kv[..., :KV_LORA], kv_norm)
    kpe = rope(rmsnorm(kv[..., KV_LORA:], kpe_norm), pos)
    return q_lat, q_pe, ckv, kpe


def mla_attend(q_lat, q_pe, q_pos, ckv, kpe, w_uv):
    key_pos = jnp.arange(ckv.shape[1], dtype=jnp.int32)
    logits = (jnp.einsum('bqhc,bkc->bhqk', q_lat, ckv).astype(jnp.float32)
              + jnp.einsum('bqhr,bkr->bhqk', q_pe, kpe).astype(jnp.float32)) * B_SCALE
    logits = jnp.where(key_pos[None, :] <= q_pos[:, None], logits, NEG)
    p = jax.nn.softmax(logits, axis=-1).astype(ckv.dtype)
    o_lat = jnp.einsum('bhqk,bkc->bqhc', p, ckv)
    o = jnp.einsum('bqhc,hcv->bqhv', o_lat, w_uv)
    return o.reshape(o.shape[0], o.shape[1], B_HEADS * V_DIM)


def mla_prompt(h, pos, w):
    q_lat, q_pe, ckv, kpe = mla_project(h, pos, w)

    def fn(ql, qp, pos_b):
        return mla_attend(ql, qp, pos_b, ckv, kpe, w[8])

    return sweep_queries(fn, (q_lat, q_pe), h.shape[1]), (ckv, kpe)


def mla_sample(h, q_pos, w, cache_ckv, cache_kpe, layer, page_table):
    q_lat, q_pe, ckv, kpe = mla_project(h, q_pos, w)
    b = h.shape[0]
    ckv_all = jnp.concatenate(
        [cache_ckv[layer, page_table].reshape(b, PAST_LEN, KV_LORA).astype(ckv.dtype), ckv], axis=1)
    kpe_all = jnp.concatenate(
        [cache_kpe[layer, page_table].reshape(b, PAST_LEN, ROPE_DIM).astype(kpe.dtype), kpe], axis=1)
    return mla_attend(q_lat, q_pe, q_pos, ckv_all, kpe_all, w[8]), (ckv, kpe)


def short_conv(h, w_in, taps, w_out, prev):
    bg, cg, u = jnp.split(h @ w_in, 3, axis=-1)
    t = h.shape[1]
    ext = jnp.concatenate([prev.astype(u.dtype), cg * u], axis=1)
    y = taps[0] * ext[:, :t]
    for j in range(1, CONV_W):
        y = y + taps[j] * ext[:, j:j + t]
    return (bg * y) @ w_out, ext[:, t:]


def swiglu(h, wg, wu, wd):
    return (jax.nn.silu(h @ wg) * (h @ wu)) @ wd


def setup_inputs(seed: int = 0) -> dict:
    key = jax.random.key(seed)
    ks = iter(jax.random.split(key, 64))
    f32 = jnp.float32
    D = D_MODEL

    def nrm(shape, scale=1.0):
        return jax.random.normal(next(ks), shape, f32) * scale

    def gain(shape):
        return 1.0 + 0.02 * jax.random.normal(next(ks), shape, f32)

    n_pages = PAST_LEN // PAGE_SIZE
    n_used = DEC_BATCH * n_pages
    n_pool = n_used + n_used // 4
    page_table = jax.random.permutation(next(ks), n_pool)[:n_used].reshape(DEC_BATCH, n_pages).astype(jnp.int32)
    hd_a = A_HEADS * A_HEAD_DIM
    hd_kv = A_KV_HEADS * A_HEAD_DIM
    return {
        'x_prompt': nrm((BATCH, SEQ, D)),
        'x_sample': nrm((DEC_BATCH, DEC_SEQ, D)),
        'cache_a_k': nrm((N_A_LAYERS, n_pool, PAGE_SIZE, A_KV_HEADS, A_HEAD_DIM)),
        'cache_a_v': nrm((N_A_LAYERS, n_pool, PAGE_SIZE, A_KV_HEADS, A_HEAD_DIM)),
        'cache_a_idx_k': nrm((N_A_LAYERS, n_pool, PAGE_SIZE, IDX_DIM)),
        'cache_mla_ckv': nrm((N_B_LAYERS, n_pool, PAGE_SIZE, KV_LORA)),
        'cache_mla_kpe': nrm((N_B_LAYERS, n_pool, PAGE_SIZE, ROPE_DIM)),
        'state_conv': nrm((N_C_LAYERS, DEC_BATCH, CONV_W - 1, D)),
        'page_table': page_table,
        'meta_tokens': nrm((N_META, D)),
        'rel_bias': nrm((N_BUCKETS, A_HEADS), 0.5),
        'norm_mix': gain((DEPTH, D)),
        'norm_ffn': gain((DEPTH, D)),
        'a_wq': nrm((N_A_LAYERS, D, hd_a), D ** -0.5),
        'a_wk': nrm((N_A_LAYERS, D, hd_kv), D ** -0.5),
        'a_wv': nrm((N_A_LAYERS, D, hd_kv), D ** -0.5),
        'a_wo': nrm((N_A_LAYERS, hd_a, D), hd_a ** -0.5),
        'a_q_norm': gain((N_A_LAYERS, A_HEAD_DIM)),
        'a_k_norm': gain((N_A_LAYERS, A_HEAD_DIM)),
        'a_w_iq': nrm((N_A_LAYERS, D, IDX_HEADS * IDX_DIM), D ** -0.5),
        'a_w_ik': nrm((N_A_LAYERS, D, IDX_DIM), D ** -0.5),
        'a_w_iw': nrm((N_A_LAYERS, D, IDX_HEADS), D ** -0.5),
        'b_w_dq': nrm((N_B_LAYERS, D, Q_LORA), D ** -0.5),
        'b_q_a_norm': gain((N_B_LAYERS, Q_LORA)),
        'b_w_uq': nrm((N_B_LAYERS, Q_LORA, B_HEADS * (NOPE_DIM + ROPE_DIM)), Q_LORA ** -0.5),
        'b_q_norm': gain((N_B_LAYERS, NOPE_DIM + ROPE_DIM)),
        'b_w_dkv': nrm((N_B_LAYERS, D, KV_LORA + ROPE_DIM), D ** -0.5),
        'b_kv_norm': gain((N_B_LAYERS, KV_LORA)),
        'b_kpe_norm': gain((N_B_LAYERS, ROPE_DIM)),
        'b_w_uk': nrm((N_B_LAYERS, B_HEADS, NOPE_DIM, KV_LORA), KV_LORA ** -0.5),
        'b_w_uv': nrm((N_B_LAYERS, B_HEADS, KV_LORA, V_DIM), KV_LORA ** -0.5),
        'b_wo': nrm((N_B_LAYERS, B_HEADS * V_DIM, D), (B_HEADS * V_DIM) ** -0.5),
        'c_w_in': nrm((N_C_LAYERS, D, 3 * D), D ** -0.5),
        'c_conv': nrm((N_C_LAYERS, CONV_W, D), CONV_W ** -0.5),
        'c_w_out': nrm((N_C_LAYERS, D, D), D ** -0.5),
        'ffn_wg': nrm((DEPTH, D, D_FF), D ** -0.5),
        'ffn_wu': nrm((DEPTH, D, D_FF), D ** -0.5),
        'ffn_wd': nrm((DEPTH, D_FF, D), D_FF ** -0.5),
    }


def reference(x_prompt, x_sample, cache_a_k, cache_a_v, cache_a_idx_k, cache_mla_ckv, cache_mla_kpe,
              state_conv, page_table, meta_tokens, rel_bias, norm_mix, norm_ffn,
              a_wq, a_wk, a_wv, a_wo, a_q_norm, a_k_norm, a_w_iq, a_w_ik, a_w_iw,
              b_w_dq, b_q_a_norm, b_w_uq, b_q_norm, b_w_dkv, b_kv_norm, b_kpe_norm, b_w_uk, b_w_uv, b_wo,
              c_w_in, c_conv, c_w_out, ffn_wg, ffn_wu, ffn_wd):
    bp = x_prompt.shape[0]
    meta = jnp.broadcast_to(meta_tokens[None].astype(x_prompt.dtype), (bp, N_META, x_prompt.shape[2]))
    xp = jnp.concatenate([meta, x_prompt], axis=1)
    xs = x_sample
    t = xp.shape[1]
    tn = xs.shape[1]
    pos_p = jnp.arange(t, dtype=jnp.int32)
    pos_s = PAST_LEN + jnp.arange(tn, dtype=jnp.int32)

    ak_p, av_p, ai_p, ak_s, av_s, ai_s = [], [], [], [], [], []
    ck_p, kp_p, ck_s, kp_s = [], [], [], []
    cv_p, cv_s = [], []
    for i in range(DEPTH):
        kind, j = i % N_MIXERS, i // N_MIXERS
        hp = rmsnorm(xp, norm_mix[i])
        hs = rmsnorm(xs, norm_mix[i])
        if kind == 0:
            w = (a_wq[j], a_wk[j], a_wv[j], a_q_norm[j], a_k_norm[j], a_w_iq[j], a_w_ik[j], a_w_iw[j])
            op, (k1, v1, i1) = dsa_prompt(hp, rel_bias, w)
            osm, (k2, v2, i2) = dsa_sample(hs, pos_s, rel_bias, w, cache_a_k, cache_a_v, cache_a_idx_k,
                                           j, page_table)
            mp, ms = op @ a_wo[j], osm @ a_wo[j]
            ak_p.append(k1); av_p.append(v1); ai_p.append(i1)
            ak_s.append(k2); av_s.append(v2); ai_s.append(i2)
        elif kind == 1:
            w = (b_w_dq[j], b_q_a_norm[j], b_w_uq[j], b_q_norm[j], b_w_dkv[j], b_kv_norm[j],
                 b_kpe_norm[j], b_w_uk[j], b_w_uv[j])
            op, (c1, e1) = mla_prompt(hp, pos_p, w)
            osm, (c2, e2) = mla_sample(hs, pos_s, w, cache_mla_ckv, cache_mla_kpe, j, page_table)
            mp, ms = op @ b_wo[j], osm @ b_wo[j]
            ck_p.append(c1); kp_p.append(e1); ck_s.append(c2); kp_s.append(e2)
        else:
            zeros = jnp.zeros((bp, CONV_W - 1, hp.shape[2]), hp.dtype)
            mp, s1 = short_conv(hp, c_w_in[j], c_conv[j], c_w_out[j], zeros)
            ms, s2 = short_conv(hs, c_w_in[j], c_conv[j], c_w_out[j], state_conv[j])
            cv_p.append(s1); cv_s.append(s2)
        xp = xp + mp
        xs = xs + ms
        xp = xp + swiglu(rmsnorm(xp, norm_ffn[i]), ffn_wg[i], ffn_wu[i], ffn_wd[i])
        xs = xs + swiglu(rmsnorm(xs, norm_ffn[i]), ffn_wg[i], ffn_wu[i], ffn_wd[i])

    y_prompt = xp[:, N_META:]
    y_sample = xs
    return (y_prompt, y_sample,
            jnp.stack(ak_p), jnp.stack(av_p), jnp.stack(ai_p),
            jnp.stack(ck_p), jnp.stack(kp_p), jnp.stack(cv_p),
            jnp.stack(ak_s), jnp.stack(av_s), jnp.stack(ai_s),
            jnp.stack(ck_s), jnp.stack(kp_s), jnp.stack(cv_s))
```

```python
import functools
import math

import jax
import jax.numpy as jnp
from jax import lax
from jax.experimental import pallas as pl
from jax.experimental.pallas import tpu as pltpu

F32 = jnp.float32
BF16 = jnp.bfloat16
I32 = jnp.int32

EPS = 1e-6
NEG = -1e30
TOPK_MAX = 256
MAX_DISTANCE = 128
ROPE_THETA = 10000.0
INT_MIN = -(2 ** 31)

LANES = 128
ATTN_TILE = 256
PAGES_PER_STEP = 8
VMEM_LIMIT = 56 * 1024 * 1024


def _cparams(n_axes):
    return pltpu.CompilerParams(dimension_semantics=("arbitrary",) * n_axes,
                                vmem_limit_bytes=VMEM_LIMIT)


def _tile(n, cap, mult=8):
    best = None
    for t in range(mult, min(n, cap) + 1, mult):
        if n % t == 0:
            best = t
    return best if best is not None else n


def _round_up(n, m):
    return (n + m - 1) // m * m


def _rmsnorm_kernel(x_ref, g_ref, o_ref):
    x = x_ref[...].astype(F32)
    ms = jnp.mean(x * x, axis=-1, keepdims=True)
    o_ref[...] = (x * lax.rsqrt(ms + EPS) * g_ref[...]).astype(o_ref.dtype)


def _rmsnorm(x, g, out_dtype):
    m, d = x.shape
    bm = _tile(m, max(16, (1 << 20) // d), 16)
    return pl.pallas_call(
        _rmsnorm_kernel,
        grid=(m // bm,),
        in_specs=[pl.BlockSpec((bm, d), lambda i: (i, 0)),
                  pl.BlockSpec((1, d), lambda i: (0, 0))],
        out_specs=pl.BlockSpec((bm, d), lambda i: (i, 0)),
        out_shape=jax.ShapeDtypeStruct((m, d), out_dtype),
        compiler_params=_cparams(1),
        name="rmsnorm",
    )(x, g.reshape(1, d).astype(F32))


def _mm_kernel(a_ref, w_ref, o_ref):
    o_ref[...] = jnp.dot(a_ref[...], w_ref[...], preferred_element_type=F32).astype(o_ref.dtype)


def _mm_res_kernel(a_ref, w_ref, r_ref, o_ref):
    acc = jnp.dot(a_ref[...], w_ref[...], preferred_element_type=F32)
    o_ref[...] = (r_ref[...] + acc).astype(o_ref.dtype)


def _matmul(a, w, residual=None, out_dtype=F32):
    m, k = a.shape
    n = w.shape[1]
    n_pad = _round_up(n, LANES)
    if n_pad != n:
        w = jnp.pad(w, ((0, 0), (0, n_pad - n)))
    bn = _tile(n_pad, max(LANES, (8 << 20) // (2 * k)), LANES)
    bm = _tile(m, 512, 16)
    in_specs = [pl.BlockSpec((bm, k), lambda j, i: (i, 0)),
                pl.BlockSpec((k, bn), lambda j, i: (0, j))]
    args = [a, w]
    body = _mm_kernel
    if residual is not None:
        assert n_pad == n
        in_specs.append(pl.BlockSpec((bm, bn), lambda j, i: (i, j)))
        args.append(residual)
        body = _mm_res_kernel
    out = pl.pallas_call(
        body,
        grid=(n_pad // bn, m // bm),
        in_specs=in_specs,
        out_specs=pl.BlockSpec((bm, bn), lambda j, i: (i, j)),
        out_shape=jax.ShapeDtypeStruct((m, n_pad), out_dtype),
        compiler_params=_cparams(2),
        name="matmul",
    )(*args)
    return out if n_pad == n else out[:, :n]


def _headmm_kernel(a_ref, w_ref, o_ref):
    o_ref[...] = jnp.dot(a_ref[...], w_ref[...], preferred_element_type=F32).astype(o_ref.dtype)


def _headmm(a, w, out_dtype):
    m = a.shape[0]
    h, k, n = w.shape
    bm = _tile(m, 1024, 16)
    return pl.pallas_call(
        _headmm_kernel,
        grid=(h, m // bm),
        in_specs=[pl.BlockSpec((bm, k), lambda j, i: (i, j)),
                  pl.BlockSpec((None, k, n), lambda j, i: (j, 0, 0))],
        out_specs=pl.BlockSpec((bm, n), lambda j, i: (i, j)),
        out_shape=jax.ShapeDtypeStruct((m, h * n), out_dtype),
        compiler_params=_cparams(2),
        name="headmm",
    )(a, w)


def _ffn_kernel(h_ref, x_ref, wg_ref, wu_ref, wd_ref, o_ref):
    f = pl.program_id(1)
    h = h_ref[...]
    g = jnp.dot(h, wg_ref[...], preferred_element_type=F32)
    u = jnp.dot(h, wu_ref[...], preferred_element_type=F32)
    a = (g * jax.nn.sigmoid(g) * u).astype(BF16)
    part = jnp.dot(a, wd_ref[...], preferred_element_type=F32)

    @pl.when(f == 0)
    def _():
        o_ref[...] = x_ref[...] + part

    @pl.when(f > 0)
    def _():
        o_ref[...] += part


def _ffn(h, x, wg, wu, wd):
    m, d = h.shape
    f = wg.shape[1]
    bm = _tile(m, 512, 16)
    bf = _tile(f, 512, LANES)
    return pl.pallas_call(
        _ffn_kernel,
        grid=(m // bm, f // bf),
        in_specs=[pl.BlockSpec((bm, d), lambda i, j: (i, 0)),
                  pl.BlockSpec((bm, d), lambda i, j: (i, 0)),
                  pl.BlockSpec((d, bf), lambda i, j: (0, j)),
                  pl.BlockSpec((d, bf), lambda i, j: (0, j)),
                  pl.BlockSpec((bf, d), lambda i, j: (j, 0))],
        out_specs=pl.BlockSpec((bm, d), lambda i, j: (i, 0)),
        out_shape=jax.ShapeDtypeStruct((m, d), F32),
        compiler_params=_cparams(2),
        name="ffn",
    )(h, x, wg, wu, wd)


def _rope_kernel(x_ref, c_ref, s_ref, o_ref):
    x = x_ref[...]
    half = x.shape[-1] // 2
    swapped = jnp.concatenate([x[..., half:], x[..., :half]], axis=-1)
    o_ref[...] = x * c_ref[...] + swapped * s_ref[...]


def _rope(x, cos2, sin2):
    m, h, r = x.shape
    bm = _tile(m, 512, 8)
    return pl.pallas_call(
        _rope_kernel,
        grid=(m // bm,),
        in_specs=[pl.BlockSpec((bm, h, r), lambda i: (i, 0, 0)),
                  pl.BlockSpec((bm, 1, r), lambda i: (i, 0, 0)),
                  pl.BlockSpec((bm, 1, r), lambda i: (i, 0, 0))],
        out_specs=pl.BlockSpec((bm, h, r), lambda i: (i, 0, 0)),
        out_shape=jax.ShapeDtypeStruct((m, h, r), F32),
        compiler_params=_cparams(1),
        name="rope",
    )(x, cos2, sin2)


def _conv_kernel(bg_ref, cg_ref, u_ref, prev_ref, taps_ref, y_ref, st_ref, ext_ref):
    t = cg_ref.shape[1]
    w = taps_ref.shape[0]
    ext_ref[:, : w - 1, :] = prev_ref[...]
    ext_ref[:, w - 1:, :] = cg_ref[...] * u_ref[...]
    y = taps_ref[0:1, :] * ext_ref[:, 0:t, :]
    for j in range(1, w):
        y = y + taps_ref[j:j + 1, :] * ext_ref[:, j:j + t, :]
    y_ref[...] = (bg_ref[...] * y).astype(y_ref.dtype)
    st_ref[...] = ext_ref[:, t:, :]


def _short_conv(p, prev, taps, bs):
    s, t, d3 = p.shape
    d = d3 // 3
    w = taps.shape[0]
    td = _tile(d, 256, LANES)
    nd = d // td
    return pl.pallas_call(
        _conv_kernel,
        grid=(s // bs, nd),
        in_specs=[pl.BlockSpec((bs, t, td), lambda i, j: (i, 0, j)),
                  pl.BlockSpec((bs, t, td), lambda i, j: (i, 0, nd + j)),
                  pl.BlockSpec((bs, t, td), lambda i, j: (i, 0, 2 * nd + j)),
                  pl.BlockSpec((bs, w - 1, td), lambda i, j: (i, 0, j)),
                  pl.BlockSpec((w, td), lambda i, j: (0, j))],
        out_specs=[pl.BlockSpec((bs, t, td), lambda i, j: (i, 0, j)),
                   pl.BlockSpec((bs, w - 1, td), lambda i, j: (i, 0, j))],
        out_shape=[jax.ShapeDtypeStruct((s, t, d), BF16),
                   jax.ShapeDtypeStruct((s, w - 1, d), F32)],
        scratch_shapes=[pltpu.VMEM((bs, t + w - 1, td), F32)],
        compiler_params=_cparams(2),
        name="short_conv",
    )(p, p, p, prev, taps)


def _sortable(x):
    bits = pltpu.bitcast(x, I32)
    bits = jnp.where(bits == jnp.int32(INT_MIN), jnp.int32(0), bits)
    return jnp.where(bits < 0, bits ^ jnp.int32(0x7FFFFFFF), bits)


def _prefix_matrix(n):
    a = lax.broadcasted_iota(I32, (n, n), 0)
    b = lax.broadcasted_iota(I32, (n, n), 1)
    return jnp.where(a <= b, 1.0, 0.0).astype(BF16)


def _topk_select(gt, eq, need, carry, umat):
    rows, n = eq.shape
    w = umat.shape[0]
    ng = n // w
    eq_f = jnp.where(eq, 1.0, 0.0).astype(BF16)
    stacked = eq_f if ng == 1 else jnp.concatenate([eq_f[:, c * w:(c + 1) * w] for c in range(ng)], axis=0)
    pre = jnp.dot(stacked, umat, preferred_element_type=F32)
    parts = []
    for c in range(ng):
        pc = pre[c * rows:(c + 1) * rows]
        take = eq[:, c * w:(c + 1) * w] & (carry + pc <= need)
        parts.append(gt[:, c * w:(c + 1) * w] | take)
        carry = carry + pc[:, w - 1:w]
    return (parts[0] if ng == 1 else jnp.concatenate(parts, axis=1)), carry


def _unsortable(k):
    return pltpu.bitcast(jnp.where(k < 0, k ^ jnp.int32(0x7FFFFFFF), k), F32)


def _kth_largest_key(count_ge, rows, k):
    kf = jnp.float32(k)
    c0 = count_ge(jnp.zeros((rows, 1), I32))
    t0 = jnp.where(c0 >= kf, jnp.int32(0), jnp.int32(INT_MIN))

    def body(bi, t):
        cand = t | jnp.left_shift(jnp.int32(1), jnp.int32(30) - bi)
        return jnp.where(count_ge(cand) >= kf, cand, t)

    return lax.fori_loop(0, 31, body, t0)


def _rel_bucket(n, n_buckets):
    max_exact = n_buckets // 2
    nf = jnp.maximum(n, 1).astype(F32)
    large = max_exact + (jnp.log(nf / max_exact) / math.log(MAX_DISTANCE / max_exact)
                         * (n_buckets - max_exact)).astype(I32)
    large = jnp.minimum(large, n_buckets - 1)
    return jnp.where(n < max_exact, n, large)


def _bias_delta(dist, relb_ref, head, n_buckets):
    bkt = _rel_bucket(jnp.maximum(dist, 0), n_buckets)
    far = relb_ref[n_buckets - 1, head]
    val = jnp.zeros(dist.shape, F32)
    for b in range(n_buckets - 1):
        val = jnp.where(bkt == b, relb_ref[b, head] - far, val)
    return jnp.where((dist >= 0) & (dist < MAX_DISTANCE), val, 0.0)


def _softmax_step(s, v, m_ref, l_ref, acc_ref, rows):
    m_prev = m_ref[rows, :]
    m_new = jnp.maximum(m_prev, jnp.max(s, axis=1, keepdims=True))
    alpha = jnp.exp(m_prev - m_new)
    p = jnp.exp(s - m_new)
    l_ref[rows, :] = alpha * l_ref[rows, :] + jnp.sum(p, axis=1, keepdims=True)
    acc_ref[rows, :] = alpha * acc_ref[rows, :] + jnp.dot(p.astype(BF16), v, preferred_element_type=F32)
    m_ref[rows, :] = m_new


def _dot_nt(a, b):
    return lax.dot_general(a, b, (((1,), (1,)), ((), ())), preferred_element_type=F32)


def _dsa_prompt_kernel(relb_ref, qi_ref, wi_ref, kit_ref, q_ref, k_ref, v_ref, o_ref,
                       key_ref, negm_ref, bias_ref, m_ref, l_ref, acc_ref, *, topk, idx_scale, a_scale):
    hi, t, _ = qi_ref.shape
    g_n, rt, dh = q_ref.shape
    r_n = rt // t
    n_buckets = relb_ref.shape[0]
    i = pl.program_id(1)
    row = lax.broadcasted_iota(I32, (t, t), 0)
    col = lax.broadcasted_iota(I32, (t, t), 1)

    @pl.when((pl.program_id(0) == 0) & (i == 0))
    def _():
        for kind in range(2):
            dist = row - col + kind * t
            for h in range(g_n * r_n):
                bias_ref[kind, h] = _bias_delta(dist, relb_ref, h, n_buckets)

    def score_tile(j, causal):
        kit = kit_ref[j]
        acc = jnp.zeros((t, t), F32)
        for h in range(hi):
            d = jnp.dot(qi_ref[h], kit, preferred_element_type=F32)
            acc = acc + jnp.maximum(d, 0.0) * wi_ref[:, h:h + 1]
        s = acc * idx_scale
        if causal:
            s = jnp.where(col <= row, s, NEG)
        key_ref[j] = _sortable(s)

    def score_body(j, c):
        score_tile(j, False)
        return c

    lax.fori_loop(0, i, score_body, 0)
    score_tile(i, True)

    def count_ge(cand):
        def body(j, acc):
            ones = jnp.where(key_ref[j] >= cand, 1.0, 0.0)
            for c in range(t // LANES):
                acc = acc + ones[:, c * LANES:(c + 1) * LANES]
            return acc
        acc = lax.fori_loop(0, i + 1, body, jnp.zeros((t, LANES), F32))
        return jnp.sum(acc, axis=1, keepdims=True)

    thr = _kth_largest_key(count_ge, t, topk)
    need = jnp.float32(topk) - count_ge(thr + 1)
    umat = _prefix_matrix(t)

    def mask_tile(j, carry, causal):
        keys = key_ref[j]
        sel, carry = _topk_select(keys > thr, keys == thr, need, carry, umat)
        if causal:
            sel = sel & (col <= row)
        negm_ref[j] = jnp.where(sel, 0.0, NEG)
        return carry

    carry = lax.fori_loop(0, i, lambda j, c: mask_tile(j, c, False), jnp.zeros((t, 1), F32))
    mask_tile(i, carry, True)

    def attend_tile(g, j, kind):
        start = pl.multiple_of(j * t, t)
        kb = k_ref[g, pl.ds(start, t), :]
        vb = v_ref[g, pl.ds(start, t), :]
        negm = negm_ref[j]
        for r in range(r_n):
            rows = pl.ds(r * t, t)
            s = _dot_nt(q_ref[g, rows, :], kb) * a_scale + negm
            if kind is not None:
                s = s + bias_ref[kind, g * r_n + r]
            _softmax_step(s, vb, m_ref, l_ref, acc_ref, rows)

    for g in range(g_n):
        m_ref[...] = jnp.full(m_ref.shape, NEG, F32)
        l_ref[...] = jnp.zeros(l_ref.shape, F32)
        acc_ref[...] = jnp.zeros(acc_ref.shape, F32)

        def plain_body(j, c, g=g):
            attend_tile(g, j, None)
            return c

        lax.fori_loop(0, i - 1, plain_body, 0)

        @pl.when(i >= 1)
        def _(g=g):
            attend_tile(g, i - 1, 1)

        attend_tile(g, i, 0)
        o_ref[g] = (acc_ref[...] / l_ref[...]).astype(o_ref.dtype)


def _dsa_prompt(rel_bias, qi, wi, kit, q, k, v, topk, idx_scale, a_scale):
    b, hi, lp, di = qi.shape
    _, g, nb, rt, dh = q.shape
    t = lp // nb
    h = g * (rt // t)
    kern = functools.partial(_dsa_prompt_kernel, topk=topk, idx_scale=idx_scale, a_scale=a_scale)
    return pl.pallas_call(
        kern,
        grid=(b, nb),
        in_specs=[pl.BlockSpec(memory_space=pltpu.SMEM),
                  pl.BlockSpec((None, hi, t, di), lambda bb, i: (bb, 0, i, 0)),
                  pl.BlockSpec((None, t, hi), lambda bb, i: (bb, i, 0)),
                  pl.BlockSpec((None, nb, di, t), lambda bb, i: (bb, 0, 0, 0)),
                  pl.BlockSpec((None, g, None, rt, dh), lambda bb, i: (bb, 0, i, 0, 0)),
                  pl.BlockSpec((None, g, lp, dh), lambda bb, i: (bb, 0, 0, 0)),
                  pl.BlockSpec((None, g, lp, dh), lambda bb, i: (bb, 0, 0, 0))],
        out_specs=pl.BlockSpec((None, g, None, rt, dh), lambda bb, i: (bb, 0, i, 0, 0)),
        out_shape=jax.ShapeDtypeStruct((b, g, nb, rt, dh), BF16),
        scratch_shapes=[pltpu.VMEM((nb, t, t), I32),
                        pltpu.VMEM((nb, t, t), F32),
                        pltpu.VMEM((2, h, t, t), F32),
                        pltpu.VMEM((rt, 1), F32),
                        pltpu.VMEM((rt, 1), F32),
                        pltpu.VMEM((rt, dh), F32)],
        compiler_params=_cparams(2),
        name="dsa_prompt",
    )(rel_bias, qi, wi, kit, q, k, v)


def _dsa_sample_index_kernel(pt_ref, qi_ref, wi_ref, kinew_ref, *rest, n_pages_step, topk, idx_scale):
    page_refs = rest[:n_pages_step]
    s_ref, snew_ref, thr_ref, need_ref, key_ref = rest[n_pages_step:]
    c = pl.program_id(1)
    nc = pl.num_programs(1)
    ht, _ = qi_ref.shape
    ts = snew_ref.shape[0]
    hi = ht // ts
    page = page_refs[0].shape[0]

    def scores(keys_bf16):
        d = _dot_nt(qi_ref[...], keys_bf16)
        d = jnp.maximum(d, 0.0) * wi_ref[...]
        acc = d[0:ts]
        for h in range(1, hi):
            acc = acc + d[h * ts:(h + 1) * ts]
        return acc * idx_scale

    for p in range(n_pages_step):
        s = scores(page_refs[p][...].astype(BF16))
        s_ref[:, p * page:(p + 1) * page] = s
        key_ref[c, :, p * page:(p + 1) * page] = _sortable(s)

    @pl.when(c == nc - 1)
    def _():
        sn = scores(kinew_ref[...].astype(BF16))
        qt = lax.broadcasted_iota(I32, (ts, LANES), 0)
        kt = lax.broadcasted_iota(I32, (ts, LANES), 1)
        sn = jnp.where(kt <= qt, sn, NEG)
        snew_ref[...] = sn
        kn = _sortable(sn)

        def count_ge(cand):
            def body(j, acc):
                blk = key_ref[j]
                ones = jnp.where(blk >= cand, 1.0, 0.0)
                for cc in range(blk.shape[1] // LANES):
                    acc = acc + ones[:, cc * LANES:(cc + 1) * LANES]
                return acc
            acc = lax.fori_loop(0, nc, body, jnp.where(kn >= cand, 1.0, 0.0))
            return jnp.sum(acc, axis=1, keepdims=True)

        tkey = _kth_largest_key(count_ge, ts, topk)
        thr = jnp.where(tkey == jnp.int32(INT_MIN), -jnp.inf, _unsortable(tkey))
        thr_ref[...] = jnp.broadcast_to(thr, thr_ref.shape)
        need_ref[...] = jnp.broadcast_to(jnp.float32(topk) - count_ge(tkey + 1), need_ref.shape)


def _dsa_sample_index(page_table, qi, wi, ki_new, cache_ik, layer, ts, topk, idx_scale):
    bs, ht, di = qi.shape
    n_pages = page_table.shape[1]
    page = cache_ik.shape[2]
    pstep = _tile(n_pages, PAGES_PER_STEP, 1)
    nc = n_pages // pstep
    past = n_pages * page

    def page_spec(p):
        return pl.BlockSpec((None, None, page, di),
                            lambda b, c, pt: (layer, pt[b, c * pstep + p], 0, 0))

    kern = functools.partial(_dsa_sample_index_kernel, n_pages_step=pstep, topk=topk, idx_scale=idx_scale)
    return pl.pallas_call(
        kern,
        grid_spec=pltpu.PrefetchScalarGridSpec(
            num_scalar_prefetch=1,
            grid=(bs, nc),
            in_specs=[pl.BlockSpec((None, ht, di), lambda b, c, pt: (b, 0, 0)),
                      pl.BlockSpec((None, ht, 1), lambda b, c, pt: (b, 0, 0)),
                      pl.BlockSpec((None, LANES, di), lambda b, c, pt: (b, 0, 0))]
                     + [page_spec(p) for p in range(pstep)],
            out_specs=[pl.BlockSpec((None, ts, pstep * page), lambda b, c, pt: (b, 0, c)),
                       pl.BlockSpec((None, ts, LANES), lambda b, c, pt: (b, 0, 0)),
                       pl.BlockSpec((None, ts, LANES), lambda b, c, pt: (b, 0, 0)),
                       pl.BlockSpec((None, ts, LANES), lambda b, c, pt: (b, 0, 0))],
            scratch_shapes=[pltpu.VMEM((nc, ts, pstep * page), I32)]),
        out_shape=[jax.ShapeDtypeStruct((bs, ts, past), F32),
                   jax.ShapeDtypeStruct((bs, ts, LANES), F32),
                   jax.ShapeDtypeStruct((bs, ts, LANES), F32),
                   jax.ShapeDtypeStruct((bs, ts, LANES), F32)],
        compiler_params=_cparams(2),
        name="dsa_sample_index",
    )(page_table, qi, wi, ki_new, *([cache_ik] * pstep))


def _dsa_sample_attn_kernel(pt_ref, relb_ref, s_ref, snew_ref, thr_ref, need_ref, q_ref, knew_ref, vnew_ref, *rest,
                            n_pages_step, a_scale):
    k_refs = rest[:n_pages_step]
    v_refs = rest[n_pages_step:2 * n_pages_step]
    o_ref, kbuf, vbuf, blast_ref, bnew_ref, carry_ref, m_ref, l_ref, acc_ref = rest[2 * n_pages_step:]
    c = pl.program_id(1)
    nc = pl.num_programs(1)
    g_n, rt, dh = q_ref.shape
    ts = snew_ref.shape[0]
    r_n = rt // ts
    page = k_refs[0].shape[0]
    n_buckets = relb_ref.shape[0]

    @pl.when((pl.program_id(0) == 0) & (c == 0))
    def _():
        qt = lax.broadcasted_iota(I32, (ts, page), 0)
        kc = lax.broadcasted_iota(I32, (ts, page), 1)
        for h in range(g_n * r_n):
            blast_ref[h * ts:(h + 1) * ts, :] = _bias_delta(page + qt - kc, relb_ref, h, n_buckets)
        qt = lax.broadcasted_iota(I32, (ts, LANES), 0)
        kc = lax.broadcasted_iota(I32, (ts, LANES), 1)
        for h in range(g_n * r_n):
            bnew_ref[h * ts:(h + 1) * ts, :] = _bias_delta(qt - kc, relb_ref, h, n_buckets)

    @pl.when(c == 0)
    def _():
        carry_ref[...] = jnp.zeros(carry_ref.shape, F32)
        m_ref[...] = jnp.full(m_ref.shape, NEG, F32)
        l_ref[...] = jnp.zeros(l_ref.shape, F32)
        acc_ref[...] = jnp.zeros(acc_ref.shape, F32)

    for p in range(n_pages_step):
        kbuf[p * page:(p + 1) * page, :] = k_refs[p][...].astype(BF16)
        vbuf[p * page:(p + 1) * page, :] = v_refs[p][...].astype(BF16)

    thr = thr_ref[:, 0:1]
    need = need_ref[:, 0:1]
    umat = _prefix_matrix(LANES)

    def select(s):
        sel, carry = _topk_select(s > thr, s == thr, need, carry_ref[...], umat)
        carry_ref[...] = carry
        return sel

    def attend(keys, vals, negm, bias, bias_lanes):
        n = keys.shape[0]
        negm_r = jnp.concatenate([negm] * r_n, axis=0)
        for g in range(g_n):
            rows = pl.ds(g * rt, rt)
            s = _dot_nt(q_ref[g], keys[:, g * dh:(g + 1) * dh]) * a_scale + negm_r
            if bias is not None:
                bg = bias[g * rt:(g + 1) * rt, :bias_lanes]
                if bias_lanes < n:
                    bg = jnp.concatenate([jnp.zeros((rt, n - bias_lanes), F32), bg], axis=1)
                s = s + bg
            _softmax_step(s, vals[:, g * dh:(g + 1) * dh], m_ref, l_ref, acc_ref, rows)

    negm_past = jnp.where(select(s_ref[...]), 0.0, NEG)

    @pl.when(c < nc - 1)
    def _():
        attend(kbuf[...], vbuf[...], negm_past, None, 0)

    @pl.when(c == nc - 1)
    def _():
        attend(kbuf[...], vbuf[...], negm_past, blast_ref[...], page)
        qt = lax.broadcasted_iota(I32, (ts, LANES), 0)
        kt = lax.broadcasted_iota(I32, (ts, LANES), 1)
        sel = select(snew_ref[...]) & (kt <= qt)
        negm_new = jnp.where(sel, 0.0, NEG)
        attend(knew_ref[...], vnew_ref[...], negm_new, bnew_ref[...], LANES)
        for g in range(g_n):
            rows = pl.ds(g * rt, rt)
            o_ref[g] = (acc_ref[rows, :] / l_ref[rows, :]).astype(o_ref.dtype)


def _dsa_sample_attn(page_table, rel_bias, s_past, s_new, thr, need, q, k_new, v_new, cache_k, cache_v, layer, a_scale):
    bs, g, rt, dh = q.shape
    ts = s_new.shape[1]
    n_pages = page_table.shape[1]
    page = cache_k.shape[2]
    pstep = _tile(n_pages, PAGES_PER_STEP, 1)
    nc = n_pages // pstep
    h = g * (rt // ts)

    def page_spec(p):
        return pl.BlockSpec((None, None, page, g * dh),
                            lambda b, c, pt: (layer, pt[b, c * pstep + p], 0, 0))

    kern = functools.partial(_dsa_sample_attn_kernel, n_pages_step=pstep, a_scale=a_scale)
    return pl.pallas_call(
        kern,
        grid_spec=pltpu.PrefetchScalarGridSpec(
            num_scalar_prefetch=1,
            grid=(bs, nc),
            in_specs=[pl.BlockSpec(memory_space=pltpu.SMEM),
                      pl.BlockSpec((None, ts, pstep * page), lambda b, c, pt: (b, 0, c)),
                      pl.BlockSpec((None, ts, LANES), lambda b, c, pt: (b, 0, 0)),
                      pl.BlockSpec((None, ts, LANES), lambda b, c, pt: (b, 0, 0)),
                      pl.BlockSpec((None, ts, LANES), lambda b, c, pt: (b, 0, 0)),
                      pl.BlockSpec((None, g, rt, dh), lambda b, c, pt: (b, 0, 0, 0)),
                      pl.BlockSpec((None, LANES, g * dh), lambda b, c, pt: (b, 0, 0)),
                      pl.BlockSpec((None, LANES, g * dh), lambda b, c, pt: (b, 0, 0))]
                     + [page_spec(p) for p in range(pstep)] * 2,
            out_specs=pl.BlockSpec((None, g, rt, dh), lambda b, c, pt: (b, 0, 0, 0)),
            scratch_shapes=[pltpu.VMEM((pstep * page, g * dh), BF16),
                            pltpu.VMEM((pstep * page, g * dh), BF16),
                            pltpu.VMEM((h * ts, page), F32),
                            pltpu.VMEM((h * ts, LANES), F32),
                            pltpu.VMEM((ts, 1), F32),
                            pltpu.VMEM((g * rt, 1), F32),
                            pltpu.VMEM((g * rt, 1), F32),
                            pltpu.VMEM((g * rt, dh), F32)]),
        out_shape=jax.ShapeDtypeStruct((bs, g, rt, dh), BF16),
        compiler_params=_cparams(2),
        name="dsa_sample_attn",
    )(page_table, rel_bias, s_past, s_new, thr, need, q, k_new, v_new,
      *([cache_k] * pstep), *([cache_v] * pstep))


def _flash_kernel(q_ref, k_ref, v_ref, o_ref, m_ref, l_ref, acc_ref, *, scale):
    t = q_ref.shape[0]
    i = pl.program_id(2)
    m_ref[...] = jnp.full(m_ref.shape, NEG, F32)
    l_ref[...] = jnp.zeros(l_ref.shape, F32)
    acc_ref[...] = jnp.zeros(acc_ref.shape, F32)
    rows = pl.ds(0, t)

    def tile(j, causal):
        start = pl.multiple_of(j * t, t)
        s = _dot_nt(q_ref[...], k_ref[pl.ds(start, t), :]) * scale
        if causal:
            row = lax.broadcasted_iota(I32, (t, t), 0)
            col = lax.broadcasted_iota(I32, (t, t), 1)
            s = jnp.where(col <= row, s, NEG)
        _softmax_step(s, v_ref[pl.ds(start, t), :], m_ref, l_ref, acc_ref, rows)

    def body(j, c):
        tile(j, False)
        return c

    lax.fori_loop(0, i, body, 0)
    tile(i, True)
    o_ref[...] = (acc_ref[...] / l_ref[...]).astype(o_ref.dtype)


def _flash(q, k, v, n_heads, t, scale):
    b, lp, hdq = q.shape
    dq = hdq // n_heads
    dv = v.shape[2] // n_heads
    nb = lp // t
    kern = functools.partial(_flash_kernel, scale=scale)
    return pl.pallas_call(
        kern,
        grid=(b, n_heads, nb),
        in_specs=[pl.BlockSpec((None, t, dq), lambda bb, h, i: (bb, i, h)),
                  pl.BlockSpec((None, lp, dq), lambda bb, h, i: (bb, 0, h)),
                  pl.BlockSpec((None, lp, dv), lambda bb, h, i: (bb, 0, h))],
        out_specs=pl.BlockSpec((None, t, dv), lambda bb, h, i: (bb, i, h)),
        out_shape=jax.ShapeDtypeStruct((b, lp, n_heads * dv), BF16),
        scratch_shapes=[pltpu.VMEM((t, 1), F32), pltpu.VMEM((t, 1), F32), pltpu.VMEM((t, dv), F32)],
        compiler_params=_cparams(3),
        name="mla_prompt_flash",
    )(q, k, v)


def _mla_sample_kernel(pt_ref, ql_ref, qp_ref, cnew_ref, pnew_ref, *rest, n_pages_step, n_heads, scale):
    c_refs = rest[:n_pages_step]
    p_refs = rest[n_pages_step:2 * n_pages_step]
    o_ref, cbuf, pbuf, m_ref, l_ref, acc_ref = rest[2 * n_pages_step:]
    c = pl.program_id(1)
    nc = pl.num_programs(1)
    rows_n = ql_ref.shape[0]
    ts = cnew_ref.shape[0]
    page = c_refs[0].shape[0]
    rows = pl.ds(0, rows_n)

    @pl.when(c == 0)
    def _():
        m_ref[...] = jnp.full(m_ref.shape, NEG, F32)
        l_ref[...] = jnp.zeros(l_ref.shape, F32)
        acc_ref[...] = jnp.zeros(acc_ref.shape, F32)

    for p in range(n_pages_step):
        cbuf[p * page:(p + 1) * page, :] = c_refs[p][...].astype(BF16)
        pbuf[p * page:(p + 1) * page, :] = p_refs[p][...].astype(BF16)

    s = (_dot_nt(ql_ref[...], cbuf[...]) + _dot_nt(qp_ref[...], pbuf[...])) * scale
    _softmax_step(s, cbuf[...], m_ref, l_ref, acc_ref, rows)

    @pl.when(c == nc - 1)
    def _():
        cn = cnew_ref[...]
        sn = (_dot_nt(ql_ref[...], cn) + _dot_nt(qp_ref[...], pnew_ref[...])) * scale
        qt = lax.broadcasted_iota(I32, (rows_n, ts), 0) // n_heads
        kt = lax.broadcasted_iota(I32, (rows_n, ts), 1)
        sn = jnp.where(kt <= qt, sn, NEG)
        _softmax_step(sn, cn, m_ref, l_ref, acc_ref, rows)
        o_ref[...] = (acc_ref[...] / l_ref[...]).astype(o_ref.dtype)


def _mla_sample(page_table, q_lat, q_pe, ckv_new, kpe_new, cache_ckv, cache_kpe, layer, n_heads, scale):
    bs, rows_n, cdim = q_lat.shape
    rdim = q_pe.shape[2]
    ts = ckv_new.shape[1]
    n_pages = page_table.shape[1]
    page = cache_ckv.shape[2]
    pstep = _tile(n_pages, PAGES_PER_STEP, 1)
    nc = n_pages // pstep

    def page_spec(p, width):
        return pl.BlockSpec((None, None, page, width),
                            lambda b, c, pt: (layer, pt[b, c * pstep + p], 0, 0))

    kern = functools.partial(_mla_sample_kernel, n_pages_step=pstep, n_heads=n_heads, scale=scale)
    return pl.pallas_call(
        kern,
        grid_spec=pltpu.PrefetchScalarGridSpec(
            num_scalar_prefetch=1,
            grid=(bs, nc),
            in_specs=[pl.BlockSpec((None, rows_n, cdim), lambda b, c, pt: (b, 0, 0)),
                      pl.BlockSpec((None, rows_n, rdim), lambda b, c, pt: (b, 0, 0)),
                      pl.BlockSpec((None, ts, cdim), lambda b, c, pt: (b, 0, 0)),
                      pl.BlockSpec((None, ts, rdim), lambda b, c, pt: (b, 0, 0))]
                     + [page_spec(p, cdim) for p in range(pstep)]
                     + [page_spec(p, rdim) for p in range(pstep)],
            out_specs=pl.BlockSpec((None, rows_n, cdim), lambda b, c, pt: (b, 0, 0)),
            scratch_shapes=[pltpu.VMEM((pstep * page, cdim), BF16),
                            pltpu.VMEM((pstep * page, rdim), BF16),
                            pltpu.VMEM((rows_n, 1), F32),
                            pltpu.VMEM((rows_n, 1), F32),
                            pltpu.VMEM((rows_n, cdim), F32)]),
        out_shape=jax.ShapeDtypeStruct((bs, rows_n, cdim), BF16),
        compiler_params=_cparams(2),
        name="mla_sample",
    )(page_table, q_lat, q_pe, ckv_new, kpe_new, *([cache_ckv] * pstep), *([cache_kpe] * pstep))


def _pad_seq(x, lp):
    pad = [(0, 0)] * x.ndim
    pad[1] = (0, lp - x.shape[1])
    return jnp.pad(x, pad)


def _dsa_layer(h, dims, page_table, rel_bias, cache_k, cache_v, cache_ik, layer, w):
    wq, wk, wv, wo, q_norm, k_norm, w_iq, w_ik, w_iw = w
    b, tp, bs, ts = dims
    mp = b * tp
    m = h.shape[0]
    dh = q_norm.shape[0]
    n_h = wq.shape[1] // dh
    g = wk.shape[1] // dh
    r = n_h // g
    di = w_ik.shape[1]
    hi = w_iw.shape[1]
    a_scale = dh ** -0.5
    idx_scale = (di * hi) ** -0.5

    q = _matmul(h, wq.astype(BF16))
    w_rest = jnp.concatenate([wk, wv, w_iq, w_ik, w_iw], axis=1).astype(BF16)
    rest = _matmul(h, w_rest)
    o1 = g * dh
    k_raw, v, qi, ki, wi = (rest[:, :o1], rest[:, o1:2 * o1], rest[:, 2 * o1:2 * o1 + hi * di],
                            rest[:, 2 * o1 + hi * di:2 * o1 + hi * di + di],
                            rest[:, 2 * o1 + hi * di + di:2 * o1 + hi * di + di + hi])
    q = _rmsnorm(q.reshape(m * n_h, dh), q_norm, BF16).reshape(m, n_h, dh)
    k = _rmsnorm(k_raw.reshape(m * g, dh), k_norm, F32).reshape(m, g, dh)
    v = v.reshape(m, g, dh)

    t = ATTN_TILE
    lp = _round_up(tp, t)
    nb = lp // t
    topk_p = min(TOPK_MAX, tp // 4)
    q_p = _pad_seq(q[:mp].reshape(b, tp, g, r, dh), lp).reshape(b, nb, t, g, r, dh)
    q_p = q_p.transpose(0, 3, 1, 4, 2, 5).reshape(b, g, nb, r * t, dh)
    k_p = _pad_seq(k[:mp].reshape(b, tp, g, dh), lp).transpose(0, 2, 1, 3).astype(BF16)
    v_p = _pad_seq(v[:mp].reshape(b, tp, g, dh), lp).transpose(0, 2, 1, 3).astype(BF16)
    qi_p = _pad_seq(qi[:mp].reshape(b, tp, hi, di), lp).transpose(0, 2, 1, 3).astype(BF16)
    wi_p = _pad_seq(wi[:mp].reshape(b, tp, hi), lp)
    kit_p = _pad_seq(ki[:mp].reshape(b, tp, di), lp).reshape(b, nb, t, di).transpose(0, 1, 3, 2).astype(BF16)
    o_p = _dsa_prompt(rel_bias, qi_p, wi_p, kit_p, q_p, k_p, v_p, topk_p, idx_scale, a_scale)
    o_p = o_p.reshape(b, g, nb, r, t, dh).transpose(0, 2, 4, 1, 3, 5).reshape(b, lp, n_h * dh)[:, :tp]

    n_pages = page_table.shape[1]
    page = cache_k.shape[2]
    topk_s = min(TOPK_MAX, (n_pages * page + ts) // 4)
    qi_s = qi[mp:].reshape(bs, ts, hi, di).transpose(0, 2, 1, 3).reshape(bs, hi * ts, di).astype(BF16)
    wi_s = wi[mp:].reshape(bs, ts, hi).transpose(0, 2, 1).reshape(bs, hi * ts, 1)
    ki_s = _pad_seq(ki[mp:].reshape(bs, ts, di), LANES)
    s_past, s_new, thr, need = _dsa_sample_index(page_table, qi_s, wi_s, ki_s, cache_ik, layer, ts, topk_s, idx_scale)
    q_s = q[mp:].reshape(bs, ts, g, r, dh).transpose(0, 2, 3, 1, 4).reshape(bs, g, r * ts, dh)
    k_s = _pad_seq(k[mp:].reshape(bs, ts, g * dh), LANES).astype(BF16)
    v_s = _pad_seq(v[mp:].reshape(bs, ts, g * dh), LANES).astype(BF16)
    ck = cache_k.reshape(cache_k.shape[0], cache_k.shape[1], page, g * dh)
    cv = cache_v.reshape(cache_v.shape[0], cache_v.shape[1], page, g * dh)
    o_s = _dsa_sample_attn(page_table, rel_bias, s_past, s_new, thr, need, q_s, k_s, v_s, ck, cv, layer, a_scale)
    o_s = o_s.reshape(bs, g, r, ts, dh).transpose(0, 3, 1, 2, 4).reshape(bs * ts, n_h * dh)

    o = jnp.concatenate([o_p.reshape(mp, n_h * dh), o_s], axis=0)
    new = ((k[:mp].reshape(b, tp, g, dh), v[:mp].reshape(b, tp, g, dh), ki[:mp].reshape(b, tp, di)),
           (k[mp:].reshape(bs, ts, g, dh), v[mp:].reshape(bs, ts, g, dh), ki[mp:].reshape(bs, ts, di)))
    return o, wo.astype(BF16), new


def _mla_layer(h, dims, cos2, sin2, page_table, cache_ckv, cache_kpe, layer, w):
    w_dq, q_a_norm, w_uq, q_norm, w_dkv, kv_norm, kpe_norm, w_uk, w_uv, wo = w
    b, tp, bs, ts = dims
    mp = b * tp
    m = h.shape[0]
    n_h, nope, clat = w_uk.shape
    vdim = w_uv.shape[2]
    qlora = w_dq.shape[1]
    rdim = kpe_norm.shape[0]
    scale = (nope + rdim) ** -0.5

    down = _matmul(h, jnp.concatenate([w_dq, w_dkv], axis=1).astype(BF16))
    qa = _rmsnorm(down[:, :qlora], q_a_norm, BF16)
    q = _matmul(qa, w_uq.astype(BF16))
    q = _rmsnorm(q.reshape(m * n_h, nope + rdim), q_norm, F32).reshape(m, n_h, nope + rdim)
    q_nope = q[..., :nope].astype(BF16)
    q_pe = _rope(q[..., nope:], cos2, sin2).astype(BF16)
    ckv = _rmsnorm(down[:, qlora:qlora + clat], kv_norm, F32)
    kpe = _rmsnorm(down[:, qlora + clat:qlora + clat + rdim], kpe_norm, F32)
    kpe = _rope(kpe.reshape(m, 1, rdim), cos2, sin2).reshape(m, rdim)
    ckv_b = ckv.astype(BF16)
    kpe_b = kpe.astype(BF16)

    t = ATTN_TILE
    lp = _round_up(tp, t)
    wuk_flat = w_uk.transpose(2, 0, 1).reshape(clat, n_h * nope).astype(BF16)
    wuv_flat = w_uv.transpose(1, 0, 2).reshape(clat, n_h * vdim).astype(BF16)
    kn = _matmul(ckv_b[:mp], wuk_flat, out_dtype=BF16).reshape(mp, n_h, nope)
    vv = _matmul(ckv_b[:mp], wuv_flat, out_dtype=BF16)
    dq = _round_up(nope + rdim, LANES)
    zpad = jnp.zeros((mp, n_h, dq - nope - rdim), BF16)
    q_cat = jnp.concatenate([q_nope[:mp], q_pe[:mp], zpad], axis=-1).reshape(b, tp, n_h * dq)
    k_cat = jnp.concatenate([kn, jnp.broadcast_to(kpe_b[:mp, None, :], (mp, n_h, rdim)), zpad], axis=-1)
    k_cat = k_cat.reshape(b, tp, n_h * dq)
    o_p = _flash(_pad_seq(q_cat, lp), _pad_seq(k_cat, lp), _pad_seq(vv.reshape(b, tp, n_h * vdim), lp),
                 n_h, t, scale)[:, :tp].reshape(mp, n_h * vdim)

    q_lat = _headmm(q_nope[mp:].reshape(bs * ts, n_h * nope), w_uk.astype(BF16), BF16)
    q_lat = q_lat.reshape(bs, ts * n_h, clat)
    q_pe_s = q_pe[mp:].reshape(bs, ts * n_h, rdim)
    o_lat = _mla_sample(page_table, q_lat, q_pe_s, _pad_seq(ckv_b[mp:].reshape(bs, ts, clat), LANES),
                        _pad_seq(kpe_b[mp:].reshape(bs, ts, rdim), LANES), cache_ckv, cache_kpe, layer, n_h, scale)
    o_s = _headmm(o_lat.reshape(bs * ts, n_h * clat), w_uv.astype(BF16), BF16)

    o = jnp.concatenate([o_p, o_s], axis=0)
    new = ((ckv[:mp].reshape(b, tp, clat), kpe[:mp].reshape(b, tp, rdim)),
           (ckv[mp:].reshape(bs, ts, clat), kpe[mp:].reshape(bs, ts, rdim)))
    return o, wo.astype(BF16), new


def _conv_layer(h, dims, state, w):
    w_in, taps, w_out = w
    b, tp, bs, ts = dims
    mp = b * tp
    d = w_out.shape[0]
    p = _matmul(h, w_in.astype(BF16))
    zeros = jnp.zeros((b, taps.shape[0] - 1, d), F32)
    y_p, st_p = _short_conv(p[:mp].reshape(b, tp, 3 * d), zeros, taps, 1)
    y_s, st_s = _short_conv(p[mp:].reshape(bs, ts, 3 * d), state, taps, _tile(bs, 128, 1))
    y = jnp.concatenate([y_p.reshape(mp, d), y_s.reshape(bs * ts, d)], axis=0)
    return y, w_out.astype(BF16), (st_p, st_s)


def kernel(x_prompt, x_sample, cache_a_k, cache_a_v, cache_a_idx_k, cache_mla_ckv, cache_mla_kpe, state_conv, page_table, meta_tokens, rel_bias, norm_mix, norm_ffn, a_wq, a_wk, a_wv, a_wo, a_q_norm, a_k_norm, a_w_iq, a_w_ik, a_w_iw, b_w_dq, b_q_a_norm, b_w_uq, b_q_norm, b_w_dkv, b_kv_norm, b_kpe_norm, b_w_uk, b_w_uv, b_wo, c_w_in, c_conv, c_w_out, ffn_wg, ffn_wu, ffn_wd):
    b, seq, d = x_prompt.shape
    bs, ts, _ = x_sample.shape
    n_meta = meta_tokens.shape[0]
    tp = seq + n_meta
    mp, ms = b * tp, bs * ts
    depth = norm_mix.shape[0]
    n_mixers = 3
    past = page_table.shape[1] * cache_a_k.shape[2]
    dims = (b, tp, bs, ts)

    meta = jnp.broadcast_to(meta_tokens[None].astype(x_prompt.dtype), (b, n_meta, d))
    x = jnp.concatenate([jnp.concatenate([meta, x_prompt], axis=1).reshape(mp, d),
                         x_sample.reshape(ms, d)], axis=0)

    rdim = b_kpe_norm.shape[1]
    half = rdim // 2
    pos = jnp.concatenate([jnp.tile(jnp.arange(tp, dtype=jnp.int32), b),
                           jnp.tile(past + jnp.arange(ts, dtype=jnp.int32), bs)])
    inv = jnp.power(ROPE_THETA, -jnp.arange(half, dtype=F32) / half)
    ang = pos.astype(F32)[:, None] * inv
    cos, sin = jnp.cos(ang), jnp.sin(ang)
    cos2 = jnp.concatenate([cos, cos], axis=-1)[:, None, :]
    sin2 = jnp.concatenate([-sin, sin], axis=-1)[:, None, :]

    a_new, b_new, c_new = [], [], []
    for i in range(depth):
        kind, j = i % n_mixers, i // n_mixers
        h = _rmsnorm(x, norm_mix[i], BF16)
        if kind == 0:
            w = (a_wq[j], a_wk[j], a_wv[j], a_wo[j], a_q_norm[j], a_k_norm[j], a_w_iq[j], a_w_ik[j], a_w_iw[j])
            o, wo, new = _dsa_layer(h, dims, page_table, rel_bias, cache_a_k, cache_a_v, cache_a_idx_k, j, w)
            a_new.append(new)
        elif kind == 1:
            w = (b_w_dq[j], b_q_a_norm[j], b_w_uq[j], b_q_norm[j], b_w_dkv[j], b_kv_norm[j], b_kpe_norm[j],
                 b_w_uk[j], b_w_uv[j], b_wo[j])
            o, wo, new = _mla_layer(h, dims, cos2, sin2, page_table, cache_mla_ckv, cache_mla_kpe, j, w)
            b_new.append(new)
        else:
            w = (c_w_in[j], c_conv[j], c_w_out[j])
            o, wo, new = _conv_layer(h, dims, state_conv[j], w)
            c_new.append(new)
        x = _matmul(o, wo, residual=x)
        hf = _rmsnorm(x, norm_ffn[i], BF16)
        x = _ffn(hf, x, ffn_wg[i].astype(BF16), ffn_wu[i].astype(BF16), ffn_wd[i].astype(BF16))

    y_prompt = x[:mp].reshape(b, tp, d)[:, n_meta:]
    y_sample = x[mp:].reshape(bs, ts, d)

    def stack(items, side, idx):
        return jnp.stack([it[side][idx] for it in items])

    return (y_prompt, y_sample,
            stack(a_new, 0, 0), stack(a_new, 0, 1), stack(a_new, 0, 2),
            stack(b_new, 0, 0), stack(b_new, 0, 1), jnp.stack([c[0] for c in c_new]),
            stack(a_new, 1, 0), stack(a_new, 1, 1), stack(a_new, 1, 2),
            stack(b_new, 1, 0), stack(b_new, 1, 1), jnp.stack([c[1] for c in c_new]))
```

```python
import functools
import math

import jax
import jax.numpy as jnp
from jax import lax
from jax.experimental import pallas as pl
from jax.experimental.pallas import tpu as pltpu

F32 = jnp.float32
BF16 = jnp.bfloat16
I32 = jnp.int32

EPS = 1e-6
NEG = -1e30
TOPK_MAX = 256
MAX_DISTANCE = 128
ROPE_THETA = 10000.0
INT_MIN = -(2 ** 31)

LANES = 128
ATTN_TILE = 256
PAGES_PER_STEP = 32
VMEM_LIMIT = 56 * 1024 * 1024


def _cparams(n_axes):
    return pltpu.CompilerParams(dimension_semantics=("arbitrary",) * n_axes,
                                vmem_limit_bytes=VMEM_LIMIT)


def _tile(n, cap, mult=8):
    best = None
    for t in range(mult, min(n, cap) + 1, mult):
        if n % t == 0:
            best = t
    return best if best is not None else n


def _round_up(n, m):
    return (n + m - 1) // m * m


def _rmsnorm_kernel(x_ref, g_ref, o_ref):
    x = x_ref[...].astype(F32)
    ms = jnp.mean(x * x, axis=-1, keepdims=True)
    o_ref[...] = (x * lax.rsqrt(ms + EPS) * g_ref[...]).astype(o_ref.dtype)


def _rmsnorm(x, g, out_dtype):
    m, d = x.shape
    bm = _tile(m, max(16, (1 << 20) // d), 16)
    return pl.pallas_call(
        _rmsnorm_kernel,
        grid=(m // bm,),
        in_specs=[pl.BlockSpec((bm, d), lambda i: (i, 0)),
                  pl.BlockSpec((1, d), lambda i: (0, 0))],
        out_specs=pl.BlockSpec((bm, d), lambda i: (i, 0)),
        out_shape=jax.ShapeDtypeStruct((m, d), out_dtype),
        compiler_params=_cparams(1),
        name="rmsnorm",
    )(x, g.reshape(1, d).astype(F32))


def _mm_kernel(a_ref, w_ref, o_ref):
    o_ref[...] = jnp.dot(a_ref[...], w_ref[...], preferred_element_type=F32).astype(o_ref.dtype)


def _mm_res_kernel(a_ref, w_ref, r_ref, o_ref):
    acc = jnp.dot(a_ref[...], w_ref[...], preferred_element_type=F32)
    o_ref[...] = (r_ref[...] + acc).astype(o_ref.dtype)


def _matmul(a, w, residual=None, out_dtype=F32):
    m, k = a.shape
    n = w.shape[1]
    n_pad = _round_up(n, LANES)
    if n_pad != n:
        w = jnp.pad(w, ((0, 0), (0, n_pad - n)))
    bn = _tile(n_pad, max(LANES, (8 << 20) // (2 * k)), LANES)
    bm = _tile(m, 512, 16)
    in_specs = [pl.BlockSpec((bm, k), lambda j, i: (i, 0)),
                pl.BlockSpec((k, bn), lambda j, i: (0, j))]
    args = [a, w]
    body = _mm_kernel
    if residual is not None:
        assert n_pad == n
        in_specs.append(pl.BlockSpec((bm, bn), lambda j, i: (i, j)))
        args.append(residual)
        body = _mm_res_kernel
    out = pl.pallas_call(
        body,
        grid=(n_pad // bn, m // bm),
        in_specs=in_specs,
        out_specs=pl.BlockSpec((bm, bn), lambda j, i: (i, j)),
        out_shape=jax.ShapeDtypeStruct((m, n_pad), out_dtype),
        compiler_params=_cparams(2),
        name="matmul",
    )(*args)
    return out if n_pad == n else out[:, :n]


def _headmm_kernel(a_ref, w_ref, o_ref):
    o_ref[...] = jnp.dot(a_ref[...], w_ref[...], preferred_element_type=F32).astype(o_ref.dtype)


def _headmm(a, w, out_dtype):
    m = a.shape[0]
    h, k, n = w.shape
    bm = _tile(m, 1024, 16)
    return pl.pallas_call(
        _headmm_kernel,
        grid=(h, m // bm),
        in_specs=[pl.BlockSpec((bm, k), lambda j, i: (i, j)),
                  pl.BlockSpec((None, k, n), lambda j, i: (j, 0, 0))],
        out_specs=pl.BlockSpec((bm, n), lambda j, i: (i, j)),
        out_shape=jax.ShapeDtypeStruct((m, h * n), out_dtype),
        compiler_params=_cparams(2),
        name="headmm",
    )(a, w)


def _ffn_kernel(h_ref, x_ref, wg_ref, wu_ref, wd_ref, o_ref):
    f = pl.program_id(1)
    h = h_ref[...]
    g = jnp.dot(h, wg_ref[...], preferred_element_type=F32)
    u = jnp.dot(h, wu_ref[...], preferred_element_type=F32)
    a = (g * jax.nn.sigmoid(g) * u).astype(BF16)
    part = jnp.dot(a, wd_ref[...], preferred_element_type=F32)

    @pl.when(f == 0)
    def _():
        o_ref[...] = x_ref[...] + part

    @pl.when(f > 0)
    def _():
        o_ref[...] += part


def _ffn(h, x, wg, wu, wd):
    m, d = h.shape
    f = wg.shape[1]
    bm = _tile(m, 512, 16)
    bf = _tile(f, 512, LANES)
    return pl.pallas_call(
        _ffn_kernel,
        grid=(m // bm, f // bf),
        in_specs=[pl.BlockSpec((bm, d), lambda i, j: (i, 0)),
                  pl.BlockSpec((bm, d), lambda i, j: (i, 0)),
                  pl.BlockSpec((d, bf), lambda i, j: (0, j)),
                  pl.BlockSpec((d, bf), lambda i, j: (0, j)),
                  pl.BlockSpec((bf, d), lambda i, j: (j, 0))],
        out_specs=pl.BlockSpec((bm, d), lambda i, j: (i, 0)),
        out_shape=jax.ShapeDtypeStruct((m, d), F32),
        compiler_params=_cparams(2),
        name="ffn",
    )(h, x, wg, wu, wd)


def _rope_kernel(x_ref, c_ref, s_ref, o_ref):
    x = x_ref[...]
    half = x.shape[-1] // 2
    swapped = jnp.concatenate([x[..., half:], x[..., :half]], axis=-1)
    o_ref[...] = x * c_ref[...] + swapped * s_ref[...]


def _rope(x, cos2, sin2):
    m, h, r = x.shape
    bm = _tile(m, 512, 8)
    return pl.pallas_call(
        _rope_kernel,
        grid=(m // bm,),
        in_specs=[pl.BlockSpec((bm, h, r), lambda i: (i, 0, 0)),
                  pl.BlockSpec((bm, 1, r), lambda i: (i, 0, 0)),
                  pl.BlockSpec((bm, 1, r), lambda i: (i, 0, 0))],
        out_specs=pl.BlockSpec((bm, h, r), lambda i: (i, 0, 0)),
        out_shape=jax.ShapeDtypeStruct((m, h, r), F32),
        compiler_params=_cparams(1),
        name="rope",
    )(x, cos2, sin2)


def _conv_kernel(bg_ref, cg_ref, u_ref, prev_ref, taps_ref, y_ref, st_ref, ext_ref):
    t = cg_ref.shape[1]
    w = taps_ref.shape[0]
    ext_ref[:, : w - 1, :] = prev_ref[...]
    ext_ref[:, w - 1:, :] = cg_ref[...] * u_ref[...]
    y = taps_ref[0:1, :] * ext_ref[:, 0:t, :]
    for j in range(1, w):
        y = y + taps_ref[j:j + 1, :] * ext_ref[:, j:j + t, :]
    y_ref[...] = (bg_ref[...] * y).astype(y_ref.dtype)
    st_ref[...] = ext_ref[:, t:, :]


def _short_conv(p, prev, taps, bs):
    s, t, d3 = p.shape
    d = d3 // 3
    w = taps.shape[0]
    td = _tile(d, 256, LANES)
    nd = d // td
    return pl.pallas_call(
        _conv_kernel,
        grid=(s // bs, nd),
        in_specs=[pl.BlockSpec((bs, t, td), lambda i, j: (i, 0, j)),
                  pl.BlockSpec((bs, t, td), lambda i, j: (i, 0, nd + j)),
                  pl.BlockSpec((bs, t, td), lambda i, j: (i, 0, 2 * nd + j)),
                  pl.BlockSpec((bs, w - 1, td), lambda i, j: (i, 0, j)),
                  pl.BlockSpec((w, td), lambda i, j: (0, j))],
        out_specs=[pl.BlockSpec((bs, t, td), lambda i, j: (i, 0, j)),
                   pl.BlockSpec((bs, w - 1, td), lambda i, j: (i, 0, j))],
        out_shape=[jax.ShapeDtypeStruct((s, t, d), BF16),
                   jax.ShapeDtypeStruct((s, w - 1, d), F32)],
        scratch_shapes=[pltpu.VMEM((bs, t + w - 1, td), F32)],
        compiler_params=_cparams(2),
        name="short_conv",
    )(p, p, p, prev, taps)


def _sortable(x):
    bits = pltpu.bitcast(x, I32)
    bits = jnp.where(bits == jnp.int32(INT_MIN), jnp.int32(0), bits)
    return jnp.where(bits < 0, bits ^ jnp.int32(0x7FFFFFFF), bits)


def _prefix_matrix(n):
    a = lax.broadcasted_iota(I32, (n, n), 0)
    b = lax.broadcasted_iota(I32, (n, n), 1)
    return jnp.where(a <= b, 1.0, 0.0).astype(BF16)


def _topk_select(gt, eq, need, carry, umat):
    rows, n = eq.shape
    w = umat.shape[0]
    ng = n // w
    eq_f = jnp.where(eq, 1.0, 0.0).astype(BF16)
    stacked = eq_f if ng == 1 else jnp.concatenate([eq_f[:, c * w:(c + 1) * w] for c in range(ng)], axis=0)
    pre = jnp.dot(stacked, umat, preferred_element_type=F32)
    parts = []
    for c in range(ng):
        pc = pre[c * rows:(c + 1) * rows]
        take = eq[:, c * w:(c + 1) * w] & (carry + pc <= need)
        parts.append(gt[:, c * w:(c + 1) * w] | take)
        carry = carry + pc[:, w - 1:w]
    return (parts[0] if ng == 1 else jnp.concatenate(parts, axis=1)), carry


def _unsortable(k):
    return pltpu.bitcast(jnp.where(k < 0, k ^ jnp.int32(0x7FFFFFFF), k), F32)


def _kth_largest_key(count_ge, rows, k):
    kf = jnp.float32(k)
    c0 = count_ge(jnp.zeros((rows, 1), I32))
    t0 = jnp.where(c0 >= kf, jnp.int32(0), jnp.int32(INT_MIN))

    def body(bi, t):
        cand = t | jnp.left_shift(jnp.int32(1), jnp.int32(30) - bi)
        return jnp.where(count_ge(cand) >= kf, cand, t)

    return lax.fori_loop(0, 31, body, t0)


def _rel_bucket(n, n_buckets):
    max_exact = n_buckets // 2
    nf = jnp.maximum(n, 1).astype(F32)
    large = max_exact + (jnp.log(nf / max_exact) / math.log(MAX_DISTANCE / max_exact)
                         * (n_buckets - max_exact)).astype(I32)
    large = jnp.minimum(large, n_buckets - 1)
    return jnp.where(n < max_exact, n, large)


def _bias_delta(dist, relb_ref, head, n_buckets):
    bkt = _rel_bucket(jnp.maximum(dist, 0), n_buckets)
    far = relb_ref[n_buckets - 1, head]
    val = jnp.zeros(dist.shape, F32)
    for b in range(n_buckets - 1):
        val = jnp.where(bkt == b, relb_ref[b, head] - far, val)
    return jnp.where((dist >= 0) & (dist < MAX_DISTANCE), val, 0.0)


def _softmax_step(s, v, m_ref, l_ref, acc_ref, rows):
    m_prev = m_ref[rows, :]
    m_new = jnp.maximum(m_prev, jnp.max(s, axis=1, keepdims=True))
    alpha = jnp.exp(m_prev - m_new)
    p = jnp.exp(s - m_new)
    l_ref[rows, :] = alpha * l_ref[rows, :] + jnp.sum(p, axis=1, keepdims=True)
    acc_ref[rows, :] = alpha * acc_ref[rows, :] + jnp.dot(p.astype(BF16), v, preferred_element_type=F32)
    m_ref[rows, :] = m_new


def _dot_nt(a, b):
    return lax.dot_general(a, b, (((1,), (1,)), ((), ())), preferred_element_type=F32)


def _dsa_prompt_kernel(relb_ref, qi_ref, wi_ref, kit_ref, q_ref, k_ref, v_ref, o_ref,
                       key_ref, negm_ref, bias_ref, m_ref, l_ref, acc_ref, *, topk, idx_scale, a_scale):
    hi, t, _ = qi_ref.shape
    g_n, rt, dh = q_ref.shape
    r_n = rt // t
    n_buckets = relb_ref.shape[0]
    i = pl.program_id(1)
    row = lax.broadcasted_iota(I32, (t, t), 0)
    col = lax.broadcasted_iota(I32, (t, t), 1)

    @pl.when((pl.program_id(0) == 0) & (i == 0))
    def _():
        for kind in range(2):
            dist = row - col + kind * t
            for h in range(g_n * r_n):
                bias_ref[kind, h] = _bias_delta(dist, relb_ref, h, n_buckets)

    def score_tile(j, causal):
        kit = kit_ref[j]
        acc = jnp.zeros((t, t), F32)
        for h in range(hi):
            d = jnp.dot(qi_ref[h], kit, preferred_element_type=F32)
            acc = acc + jnp.maximum(d, 0.0) * wi_ref[:, h:h + 1]
        s = acc * idx_scale
        if causal:
            s = jnp.where(col <= row, s, NEG)
        key_ref[j] = _sortable(s)

    def score_body(j, c):
        score_tile(j, False)
        return c

    lax.fori_loop(0, i, score_body, 0)
    score_tile(i, True)

    def count_ge(cand):
        def body(j, acc):
            ones = jnp.where(key_ref[j] >= cand, 1.0, 0.0)
            for c in range(t // LANES):
                acc = acc + ones[:, c * LANES:(c + 1) * LANES]
            return acc
        acc = lax.fori_loop(0, i + 1, body, jnp.zeros((t, LANES), F32))
        return jnp.sum(acc, axis=1, keepdims=True)

    thr = _kth_largest_key(count_ge, t, topk)
    need = jnp.float32(topk) - count_ge(thr + 1)
    umat = _prefix_matrix(t)

    def mask_tile(j, carry, causal):
        keys = key_ref[j]
        sel, carry = _topk_select(keys > thr, keys == thr, need, carry, umat)
        if causal:
            sel = sel & (col <= row)
        negm_ref[j] = jnp.where(sel, 0.0, NEG)
        return carry

    carry = lax.fori_loop(0, i, lambda j, c: mask_tile(j, c, False), jnp.zeros((t, 1), F32))
    mask_tile(i, carry, True)

    def attend_tile(g, j, kind):
        start = pl.multiple_of(j * t, t)
        kb = k_ref[g, pl.ds(start, t), :]
        vb = v_ref[g, pl.ds(start, t), :]
        negm = negm_ref[j]
        for r in range(r_n):
            rows = pl.ds(r * t, t)
            s = _dot_nt(q_ref[g, rows, :], kb) * a_scale + negm
            if kind is not None:
                s = s + bias_ref[kind, g * r_n + r]
            _softmax_step(s, vb, m_ref, l_ref, acc_ref, rows)

    for g in range(g_n):
        m_ref[...] = jnp.full(m_ref.shape, NEG, F32)
        l_ref[...] = jnp.zeros(l_ref.shape, F32)
        acc_ref[...] = jnp.zeros(acc_ref.shape, F32)

        def plain_body(j, c, g=g):
            attend_tile(g, j, None)
            return c

        lax.fori_loop(0, i - 1, plain_body, 0)

        @pl.when(i >= 1)
        def _(g=g):
            attend_tile(g, i - 1, 1)

        attend_tile(g, i, 0)
        o_ref[g] = (acc_ref[...] / l_ref[...]).astype(o_ref.dtype)


def _dsa_prompt(rel_bias, qi, wi, kit, q, k, v, topk, idx_scale, a_scale):
    b, hi, lp, di = qi.shape
    _, g, nb, rt, dh = q.shape
    t = lp // nb
    h = g * (rt // t)
    kern = functools.partial(_dsa_prompt_kernel, topk=topk, idx_scale=idx_scale, a_scale=a_scale)
    return pl.pallas_call(
        kern,
        grid=(b, nb),
        in_specs=[pl.BlockSpec(memory_space=pltpu.SMEM),
                  pl.BlockSpec((None, hi, t, di), lambda bb, i: (bb, 0, i, 0)),
                  pl.BlockSpec((None, t, hi), lambda bb, i: (bb, i, 0)),
                  pl.BlockSpec((None, nb, di, t), lambda bb, i: (bb, 0, 0, 0)),
                  pl.BlockSpec((None, g, None, rt, dh), lambda bb, i: (bb, 0, i, 0, 0)),
                  pl.BlockSpec((None, g, lp, dh), lambda bb, i: (bb, 0, 0, 0)),
                  pl.BlockSpec((None, g, lp, dh), lambda bb, i: (bb, 0, 0, 0))],
        out_specs=pl.BlockSpec((None, g, None, rt, dh), lambda bb, i: (bb, 0, i, 0, 0)),
        out_shape=jax.ShapeDtypeStruct((b, g, nb, rt, dh), BF16),
        scratch_shapes=[pltpu.VMEM((nb, t, t), I32),
                        pltpu.VMEM((nb, t, t), F32),
                        pltpu.VMEM((2, h, t, t), F32),
                        pltpu.VMEM((rt, 1), F32),
                        pltpu.VMEM((rt, 1), F32),
                        pltpu.VMEM((rt, dh), F32)],
        compiler_params=_cparams(2),
        name="dsa_prompt",
    )(rel_bias, qi, wi, kit, q, k, v)


def _dsa_sample_index_kernel(pt_ref, qi_ref, wi_ref, kinew_ref, *rest, n_pages_step, idx_scale):
    page_refs = rest[:n_pages_step]
    s_ref, snew_ref, kt_ref = rest[n_pages_step:]
    ht, _ = qi_ref.shape
    ts = snew_ref.shape[0]
    hi = ht // ts
    page = page_refs[0].shape[1]
    chunk = _tile(n_pages_step * page, 8 * page, page)

    def head_sum(d):
        d = jnp.maximum(d, 0.0) * wi_ref[...]
        acc = d[0:ts]
        for h in range(1, hi):
            acc = acc + d[h * ts:(h + 1) * ts]
        return acc * idx_scale

    for p in range(n_pages_step):
        kt_ref[:, p * page:(p + 1) * page] = page_refs[p][...].astype(BF16)
    for c in range(n_pages_step * page // chunk):
        lanes = slice(c * chunk, (c + 1) * chunk)
        s_ref[:, lanes] = head_sum(jnp.dot(qi_ref[...], kt_ref[:, lanes], preferred_element_type=F32))

    sn = head_sum(_dot_nt(qi_ref[...], kinew_ref[...].astype(BF16)))
    qt = lax.broadcasted_iota(I32, (ts, LANES), 0)
    kt = lax.broadcasted_iota(I32, (ts, LANES), 1)
    snew_ref[...] = jnp.where(kt <= qt, sn, NEG)


def _dsa_sample_index(page_table, qi, wi, ki_new, cache_ikt, layer, ts, idx_scale):
    bs, ht, di = qi.shape
    n_pages = page_table.shape[1]
    page = cache_ikt.shape[3]
    past = n_pages * page

    def page_spec(p):
        return pl.BlockSpec((None, None, di, page), lambda b, pt: (layer, pt[b, p], 0, 0))

    kern = functools.partial(_dsa_sample_index_kernel, n_pages_step=n_pages, idx_scale=idx_scale)
    return pl.pallas_call(
        kern,
        grid_spec=pltpu.PrefetchScalarGridSpec(
            num_scalar_prefetch=1,
            grid=(bs,),
            in_specs=[pl.BlockSpec((None, ht, di), lambda b, pt: (b, 0, 0)),
                      pl.BlockSpec((None, ht, 1), lambda b, pt: (b, 0, 0)),
                      pl.BlockSpec((None, LANES, di), lambda b, pt: (b, 0, 0))]
                     + [page_spec(p) for p in range(n_pages)],
            out_specs=[pl.BlockSpec((None, ts, past), lambda b, pt: (b, 0, 0)),
                       pl.BlockSpec((None, ts, LANES), lambda b, pt: (b, 0, 0))],
            scratch_shapes=[pltpu.VMEM((di, past), BF16)]),
        out_shape=[jax.ShapeDtypeStruct((bs, ts, past), F32),
                   jax.ShapeDtypeStruct((bs, ts, LANES), F32)],
        compiler_params=_cparams(1),
        name="dsa_sample_index",
    )(page_table, qi, wi, ki_new, *([cache_ikt] * n_pages))


def _topk_threshold_kernel(s_ref, snew_ref, thr_ref, need_ref, key_ref, *, topk):
    rows = s_ref.shape[0]
    n_chunks, _, chunk = key_ref.shape
    for c in range(n_chunks):
        key_ref[c] = _sortable(s_ref[:, c * chunk:(c + 1) * chunk])
    kn = _sortable(snew_ref[...])

    def count_ge(cand):
        def body(j, acc):
            ones = jnp.where(key_ref[j] >= cand, 1.0, 0.0)
            for cc in range(chunk // LANES):
                acc = acc + ones[:, cc * LANES:(cc + 1) * LANES]
            return acc
        acc = lax.fori_loop(0, n_chunks, body, jnp.where(kn >= cand, 1.0, 0.0))
        return jnp.sum(acc, axis=1, keepdims=True)

    tkey = _kth_largest_key(count_ge, rows, topk)
    thr = jnp.where(tkey == jnp.int32(INT_MIN), -jnp.inf, _unsortable(tkey))
    thr_ref[...] = jnp.broadcast_to(thr, thr_ref.shape)
    need_ref[...] = jnp.broadcast_to(jnp.float32(topk) - count_ge(tkey + 1), need_ref.shape)


def _topk_threshold(s_past, s_new, topk):
    m, past = s_past.shape
    bm = _tile(m, 128, 8)
    chunk = _tile(past, 8 * LANES, LANES)
    kern = functools.partial(_topk_threshold_kernel, topk=topk)
    return pl.pallas_call(
        kern,
        grid=(m // bm,),
        in_specs=[pl.BlockSpec((bm, past), lambda i: (i, 0)),
                  pl.BlockSpec((bm, LANES), lambda i: (i, 0))],
        out_specs=[pl.BlockSpec((bm, LANES), lambda i: (i, 0)),
                   pl.BlockSpec((bm, LANES), lambda i: (i, 0))],
        out_shape=[jax.ShapeDtypeStruct((m, LANES), F32),
                   jax.ShapeDtypeStruct((m, LANES), F32)],
        scratch_shapes=[pltpu.VMEM((past // chunk, bm, chunk), I32)],
        compiler_params=_cparams(1),
        name="topk_threshold",
    )(s_past, s_new)


def _dsa_sample_attn_kernel(pt_ref, relb_ref, s_ref, snew_ref, thr_ref, need_ref, q_ref, knew_ref, vnew_ref, *rest,
                            n_pages_step, a_scale):
    k_refs = rest[:n_pages_step]
    v_refs = rest[n_pages_step:2 * n_pages_step]
    o_ref, kbuf, vbuf, blast_ref, bnew_ref, carry_ref, m_ref, l_ref, acc_ref = rest[2 * n_pages_step:]
    c = pl.program_id(1)
    nc = pl.num_programs(1)
    g_n, rt, dh = q_ref.shape
    ts = snew_ref.shape[0]
    r_n = rt // ts
    page = k_refs[0].shape[0] // g_n
    n_buckets = relb_ref.shape[0]

    @pl.when((pl.program_id(0) == 0) & (c == 0))
    def _():
        qt = lax.broadcasted_iota(I32, (ts, page), 0)
        kc = lax.broadcasted_iota(I32, (ts, page), 1)
        for h in range(g_n * r_n):
            blast_ref[h * ts:(h + 1) * ts, :] = _bias_delta(page + qt - kc, relb_ref, h, n_buckets)
        qt = lax.broadcasted_iota(I32, (ts, LANES), 0)
        kc = lax.broadcasted_iota(I32, (ts, LANES), 1)
        for h in range(g_n * r_n):
            bnew_ref[h * ts:(h + 1) * ts, :] = _bias_delta(qt - kc, relb_ref, h, n_buckets)

    @pl.when(c == 0)
    def _():
        carry_ref[...] = jnp.zeros(carry_ref.shape, F32)
        m_ref[...] = jnp.full(m_ref.shape, NEG, F32)
        l_ref[...] = jnp.zeros(l_ref.shape, F32)
        acc_ref[...] = jnp.zeros(acc_ref.shape, F32)

    for p in range(n_pages_step):
        for g in range(g_n):
            kbuf[g, p * page:(p + 1) * page, :] = k_refs[p][pl.ds(g, page, stride=g_n), :].astype(BF16)
            vbuf[g, p * page:(p + 1) * page, :] = v_refs[p][pl.ds(g, page, stride=g_n), :].astype(BF16)

    thr = thr_ref[:, 0:1]
    need = need_ref[:, 0:1]
    umat = _prefix_matrix(LANES)

    def select(s):
        sel, carry = _topk_select(s > thr, s == thr, need, carry_ref[...], umat)
        carry_ref[...] = carry
        return sel

    def attend(keys, vals, negm, bias, bias_lanes):
        n = negm.shape[1]
        negm_r = jnp.concatenate([negm] * r_n, axis=0)
        for g in range(g_n):
            rows = pl.ds(g * rt, rt)
            s = _dot_nt(q_ref[g], keys(g)) * a_scale + negm_r
            if bias is not None:
                bg = bias[g * rt:(g + 1) * rt, :bias_lanes]
                if bias_lanes < n:
                    bg = jnp.concatenate([jnp.zeros((rt, n - bias_lanes), F32), bg], axis=1)
                s = s + bg
            _softmax_step(s, vals(g), m_ref, l_ref, acc_ref, rows)

    negm_past = jnp.where(select(s_ref[...]), 0.0, NEG)

    @pl.when(c < nc - 1)
    def _():
        attend(lambda g: kbuf[g], lambda g: vbuf[g], negm_past, None, 0)

    @pl.when(c == nc - 1)
    def _():
        attend(lambda g: kbuf[g], lambda g: vbuf[g], negm_past, blast_ref[...], page)
        qt = lax.broadcasted_iota(I32, (ts, LANES), 0)
        kt = lax.broadcasted_iota(I32, (ts, LANES), 1)
        sel = select(snew_ref[...]) & (kt <= qt)
        negm_new = jnp.where(sel, 0.0, NEG)
        attend(lambda g: knew_ref[:, g * dh:(g + 1) * dh], lambda g: vnew_ref[:, g * dh:(g + 1) * dh],
               negm_new, bnew_ref[...], LANES)
        for g in range(g_n):
            rows = pl.ds(g * rt, rt)
            o_ref[g] = (acc_ref[rows, :] / l_ref[rows, :]).astype(o_ref.dtype)


def _dsa_sample_attn(page_table, rel_bias, s_past, s_new, thr, need, q, k_new, v_new, cache_k, cache_v, layer, a_scale):
    bs, g, rt, dh = q.shape
    ts = s_new.shape[1]
    n_pages = page_table.shape[1]
    page = cache_k.shape[2] // g
    pstep = _tile(n_pages, PAGES_PER_STEP, 1)
    nc = n_pages // pstep
    h = g * (rt // ts)

    def page_spec(p):
        return pl.BlockSpec((None, None, page * g, dh),
                            lambda b, c, pt: (layer, pt[b, c * pstep + p], 0, 0))

    kern = functools.partial(_dsa_sample_attn_kernel, n_pages_step=pstep, a_scale=a_scale)
    return pl.pallas_call(
        kern,
        grid_spec=pltpu.PrefetchScalarGridSpec(
            num_scalar_prefetch=1,
            grid=(bs, nc),
            in_specs=[pl.BlockSpec(memory_space=pltpu.SMEM),
                      pl.BlockSpec((None, ts, pstep * page), lambda b, c, pt: (b, 0, c)),
                      pl.BlockSpec((None, ts, LANES), lambda b, c, pt: (b, 0, 0)),
                      pl.BlockSpec((None, ts, LANES), lambda b, c, pt: (b, 0, 0)),
                      pl.BlockSpec((None, ts, LANES), lambda b, c, pt: (b, 0, 0)),
                      pl.BlockSpec((None, g, rt, dh), lambda b, c, pt: (b, 0, 0, 0)),
                      pl.BlockSpec((None, LANES, g * dh), lambda b, c, pt: (b, 0, 0)),
                      pl.BlockSpec((None, LANES, g * dh), lambda b, c, pt: (b, 0, 0))]
                     + [page_spec(p) for p in range(pstep)] * 2,
            out_specs=pl.BlockSpec((None, g, rt, dh), lambda b, c, pt: (b, 0, 0, 0)),
            scratch_shapes=[pltpu.VMEM((g, pstep * page, dh), BF16),
                            pltpu.VMEM((g, pstep * page, dh), BF16),
                            pltpu.VMEM((h * ts, page), F32),
                            pltpu.VMEM((h * ts, LANES), F32),
                            pltpu.VMEM((ts, 1), F32),
                            pltpu.VMEM((g * rt, 1), F32),
                            pltpu.VMEM((g * rt, 1), F32),
                            pltpu.VMEM((g * rt, dh), F32)]),
        out_shape=jax.ShapeDtypeStruct((bs, g, rt, dh), BF16),
        compiler_params=_cparams(2),
        name="dsa_sample_attn",
    )(page_table, rel_bias, s_past, s_new, thr, need, q, k_new, v_new,
      *([cache_k] * pstep), *([cache_v] * pstep))


def _flash_kernel(q_ref, k_ref, v_ref, o_ref, m_ref, l_ref, acc_ref, *, scale):
    t = q_ref.shape[0]
    i = pl.program_id(2)
    m_ref[...] = jnp.full(m_ref.shape, NEG, F32)
    l_ref[...] = jnp.zeros(l_ref.shape, F32)
    acc_ref[...] = jnp.zeros(acc_ref.shape, F32)
    rows = pl.ds(0, t)

    def tile(j, causal):
        start = pl.multiple_of(j * t, t)
        s = _dot_nt(q_ref[...], k_ref[pl.ds(start, t), :]) * scale
        if causal:
            row = lax.broadcasted_iota(I32, (t, t), 0)
            col = lax.broadcasted_iota(I32, (t, t), 1)
            s = jnp.where(col <= row, s, NEG)
        _softmax_step(s, v_ref[pl.ds(start, t), :], m_ref, l_ref, acc_ref, rows)

    def body(j, c):
        tile(j, False)
        return c

    lax.fori_loop(0, i, body, 0)
    tile(i, True)
    o_ref[...] = (acc_ref[...] / l_ref[...]).astype(o_ref.dtype)


def _flash(q, k, v, n_heads, t, scale):
    b, lp, hdq = q.shape
    dq = hdq // n_heads
    dv = v.shape[2] // n_heads
    nb = lp // t
    kern = functools.partial(_flash_kernel, scale=scale)
    return pl.pallas_call(
        kern,
        grid=(b, n_heads, nb),
        in_specs=[pl.BlockSpec((None, t, dq), lambda bb, h, i: (bb, i, h)),
                  pl.BlockSpec((None, lp, dq), lambda bb, h, i: (bb, 0, h)),
                  pl.BlockSpec((None, lp, dv), lambda bb, h, i: (bb, 0, h))],
        out_specs=pl.BlockSpec((None, t, dv), lambda bb, h, i: (bb, i, h)),
        out_shape=jax.ShapeDtypeStruct((b, lp, n_heads * dv), BF16),
        scratch_shapes=[pltpu.VMEM((t, 1), F32), pltpu.VMEM((t, 1), F32), pltpu.VMEM((t, dv), F32)],
        compiler_params=_cparams(3),
        name="mla_prompt_flash",
    )(q, k, v)


def _mla_sample_kernel(pt_ref, ql_ref, qp_ref, cnew_ref, pnew_ref, *rest, n_pages_step, n_heads, scale):
    c_refs = rest[:n_pages_step]
    p_refs = rest[n_pages_step:2 * n_pages_step]
    o_ref, cbuf, pbuf, m_ref, l_ref, acc_ref = rest[2 * n_pages_step:]
    c = pl.program_id(1)
    nc = pl.num_programs(1)
    rows_n = ql_ref.shape[0]
    ts = cnew_ref.shape[0]
    page = c_refs[0].shape[0]
    rows = pl.ds(0, rows_n)

    @pl.when(c == 0)
    def _():
        m_ref[...] = jnp.full(m_ref.shape, NEG, F32)
        l_ref[...] = jnp.zeros(l_ref.shape, F32)
        acc_ref[...] = jnp.zeros(acc_ref.shape, F32)

    for p in range(n_pages_step):
        cbuf[p * page:(p + 1) * page, :] = c_refs[p][...].astype(BF16)
        pbuf[:, p * page:(p + 1) * page] = p_refs[p][...].astype(BF16)

    s = (_dot_nt(ql_ref[...], cbuf[...])
         + jnp.dot(qp_ref[...], pbuf[...], preferred_element_type=F32)) * scale
    _softmax_step(s, cbuf[...], m_ref, l_ref, acc_ref, rows)

    @pl.when(c == nc - 1)
    def _():
        cn = cnew_ref[...]
        sn = (_dot_nt(ql_ref[...], cn) + _dot_nt(qp_ref[...], pnew_ref[...])) * scale
        qt = lax.broadcasted_iota(I32, (rows_n, ts), 0) // n_heads
        kt = lax.broadcasted_iota(I32, (rows_n, ts), 1)
        sn = jnp.where(kt <= qt, sn, NEG)
        _softmax_step(sn, cn, m_ref, l_ref, acc_ref, rows)
        o_ref[...] = (acc_ref[...] / l_ref[...]).astype(o_ref.dtype)


def _mla_sample(page_table, q_lat, q_pe, ckv_new, kpe_new, cache_ckv, cache_kpe, layer, n_heads, scale):
    bs, rows_n, cdim = q_lat.shape
    rdim = q_pe.shape[2]
    ts = ckv_new.shape[1]
    n_pages = page_table.shape[1]
    page = cache_ckv.shape[2]
    pstep = _tile(n_pages, PAGES_PER_STEP, 1)
    nc = n_pages // pstep

    def page_spec(p, shape):
        return pl.BlockSpec((None, None) + shape,
                            lambda b, c, pt: (layer, pt[b, c * pstep + p], 0, 0))

    kern = functools.partial(_mla_sample_kernel, n_pages_step=pstep, n_heads=n_heads, scale=scale)
    return pl.pallas_call(
        kern,
        grid_spec=pltpu.PrefetchScalarGridSpec(
            num_scalar_prefetch=1,
            grid=(bs, nc),
            in_specs=[pl.BlockSpec((None, rows_n, cdim), lambda b, c, pt: (b, 0, 0)),
                      pl.BlockSpec((None, rows_n, rdim), lambda b, c, pt: (b, 0, 0)),
                      pl.BlockSpec((None, ts, cdim), lambda b, c, pt: (b, 0, 0)),
                      pl.BlockSpec((None, ts, rdim), lambda b, c, pt: (b, 0, 0))]
                     + [page_spec(p, (page, cdim)) for p in range(pstep)]
                     + [page_spec(p, (rdim, page)) for p in range(pstep)],
            out_specs=pl.BlockSpec((None, rows_n, cdim), lambda b, c, pt: (b, 0, 0)),
            scratch_shapes=[pltpu.VMEM((pstep * page, cdim), BF16),
                            pltpu.VMEM((rdim, pstep * page), BF16),
                            pltpu.VMEM((rows_n, 1), F32),
                            pltpu.VMEM((rows_n, 1), F32),
                            pltpu.VMEM((rows_n, cdim), F32)]),
        out_shape=jax.ShapeDtypeStruct((bs, rows_n, cdim), BF16),
        compiler_params=_cparams(2),
        name="mla_sample",
    )(page_table, q_lat, q_pe, ckv_new, kpe_new, *([cache_ckv] * pstep), *([cache_kpe] * pstep))


def _pad_seq(x, lp):
    pad = [(0, 0)] * x.ndim
    pad[1] = (0, lp - x.shape[1])
    return jnp.pad(x, pad)


def _dsa_layer(h, dims, page_table, rel_bias, cache_k, cache_v, cache_ik, layer, w):
    wq, wk, wv, wo, q_norm, k_norm, w_iq, w_ik, w_iw = w
    b, tp, bs, ts = dims
    mp = b * tp
    m = h.shape[0]
    dh = q_norm.shape[0]
    n_h = wq.shape[1] // dh
    g = wk.shape[1] // dh
    r = n_h // g
    di = w_ik.shape[1]
    hi = w_iw.shape[1]
    a_scale = dh ** -0.5
    idx_scale = (di * hi) ** -0.5

    q = _matmul(h, wq.astype(BF16))
    w_rest = jnp.concatenate([wk, wv, w_iq, w_ik, w_iw], axis=1).astype(BF16)
    rest = _matmul(h, w_rest)
    o1 = g * dh
    k_raw, v, qi, ki, wi = (rest[:, :o1], rest[:, o1:2 * o1], rest[:, 2 * o1:2 * o1 + hi * di],
                            rest[:, 2 * o1 + hi * di:2 * o1 + hi * di + di],
                            rest[:, 2 * o1 + hi * di + di:2 * o1 + hi * di + di + hi])
    q = _rmsnorm(q.reshape(m * n_h, dh), q_norm, BF16).reshape(m, n_h, dh)
    k = _rmsnorm(k_raw.reshape(m * g, dh), k_norm, F32).reshape(m, g, dh)
    v = v.reshape(m, g, dh)

    t = ATTN_TILE
    lp = _round_up(tp, t)
    nb = lp // t
    topk_p = min(TOPK_MAX, tp // 4)
    q_p = _pad_seq(q[:mp].reshape(b, tp, g, r, dh), lp).reshape(b, nb, t, g, r, dh)
    q_p = q_p.transpose(0, 3, 1, 4, 2, 5).reshape(b, g, nb, r * t, dh)
    k_p = _pad_seq(k[:mp].reshape(b, tp, g, dh), lp).transpose(0, 2, 1, 3).astype(BF16)
    v_p = _pad_seq(v[:mp].reshape(b, tp, g, dh), lp).transpose(0, 2, 1, 3).astype(BF16)
    qi_p = _pad_seq(qi[:mp].reshape(b, tp, hi, di), lp).transpose(0, 2, 1, 3).astype(BF16)
    wi_p = _pad_seq(wi[:mp].reshape(b, tp, hi), lp)
    kit_p = _pad_seq(ki[:mp].reshape(b, tp, di), lp).reshape(b, nb, t, di).transpose(0, 1, 3, 2).astype(BF16)
    o_p = _dsa_prompt(rel_bias, qi_p, wi_p, kit_p, q_p, k_p, v_p, topk_p, idx_scale, a_scale)
    o_p = o_p.reshape(b, g, nb, r, t, dh).transpose(0, 2, 4, 1, 3, 5).reshape(b, lp, n_h * dh)[:, :tp]

    n_pages = page_table.shape[1]
    page = cache_k.shape[2]
    topk_s = min(TOPK_MAX, (n_pages * page + ts) // 4)
    qi_s = qi[mp:].reshape(bs, ts, hi, di).transpose(0, 2, 1, 3).reshape(bs, hi * ts, di).astype(BF16)
    wi_s = wi[mp:].reshape(bs, ts, hi).transpose(0, 2, 1).reshape(bs, hi * ts, 1)
    ki_s = _pad_seq(ki[mp:].reshape(bs, ts, di), LANES)
    s_past, s_new = _dsa_sample_index(page_table, qi_s, wi_s, ki_s, jnp.swapaxes(cache_ik, 2, 3), layer, ts, idx_scale)
    thr, need = _topk_threshold(s_past.reshape(bs * ts, -1), s_new.reshape(bs * ts, LANES), topk_s)
    thr, need = thr.reshape(bs, ts, LANES), need.reshape(bs, ts, LANES)
    q_s = q[mp:].reshape(bs, ts, g, r, dh).transpose(0, 2, 3, 1, 4).reshape(bs, g, r * ts, dh)
    k_s = _pad_seq(k[mp:].reshape(bs, ts, g * dh), LANES).astype(BF16)
    v_s = _pad_seq(v[mp:].reshape(bs, ts, g * dh), LANES).astype(BF16)
    ck = cache_k.reshape(cache_k.shape[0], cache_k.shape[1], page * g, dh)
    cv = cache_v.reshape(cache_v.shape[0], cache_v.shape[1], page * g, dh)
    o_s = _dsa_sample_attn(page_table, rel_bias, s_past, s_new, thr, need, q_s, k_s, v_s, ck, cv, layer, a_scale)
    o_s = o_s.reshape(bs, g, r, ts, dh).transpose(0, 3, 1, 2, 4).reshape(bs * ts, n_h * dh)

    o = jnp.concatenate([o_p.reshape(mp, n_h * dh), o_s], axis=0)
    new = ((k[:mp].reshape(b, tp, g, dh), v[:mp].reshape(b, tp, g, dh), ki[:mp].reshape(b, tp, di)),
           (k[mp:].reshape(bs, ts, g, dh), v[mp:].reshape(bs, ts, g, dh), ki[mp:].reshape(bs, ts, di)))
    return o, wo.astype(BF16), new


def _mla_layer(h, dims, cos2, sin2, page_table, cache_ckv, cache_kpe, layer, w):
    w_dq, q_a_norm, w_uq, q_norm, w_dkv, kv_norm, kpe_norm, w_uk, w_uv, wo = w
    b, tp, bs, ts = dims
    mp = b * tp
    m = h.shape[0]
    n_h, nope, clat = w_uk.shape
    vdim = w_uv.shape[2]
    qlora = w_dq.shape[1]
    rdim = kpe_norm.shape[0]
    scale = (nope + rdim) ** -0.5

    down = _matmul(h, jnp.concatenate([w_dq, w_dkv], axis=1).astype(BF16))
    qa = _rmsnorm(down[:, :qlora], q_a_norm, BF16)
    q = _matmul(qa, w_uq.astype(BF16))
    q = _rmsnorm(q.reshape(m * n_h, nope + rdim), q_norm, F32).reshape(m, n_h, nope + rdim)
    q_nope = q[..., :nope].astype(BF16)
    q_pe = _rope(q[..., nope:], cos2, sin2).astype(BF16)
    ckv = _rmsnorm(down[:, qlora:qlora + clat], kv_norm, F32)
    kpe = _rmsnorm(down[:, qlora + clat:qlora + clat + rdim], kpe_norm, F32)
    kpe = _rope(kpe.reshape(m, 1, rdim), cos2, sin2).reshape(m, rdim)
    ckv_b = ckv.astype(BF16)
    kpe_b = kpe.astype(BF16)

    t = ATTN_TILE
    lp = _round_up(tp, t)
    wuk_flat = w_uk.transpose(2, 0, 1).reshape(clat, n_h * nope).astype(BF16)
    wuv_flat = w_uv.transpose(1, 0, 2).reshape(clat, n_h * vdim).astype(BF16)
    kn = _matmul(ckv_b[:mp], wuk_flat, out_dtype=BF16).reshape(mp, n_h, nope)
    vv = _matmul(ckv_b[:mp], wuv_flat, out_dtype=BF16)
    dq = _round_up(nope + rdim, LANES)
    zpad = jnp.zeros((mp, n_h, dq - nope - rdim), BF16)
    q_cat = jnp.concatenate([q_nope[:mp], q_pe[:mp], zpad], axis=-1).reshape(b, tp, n_h * dq)
    k_cat = jnp.concatenate([kn, jnp.broadcast_to(kpe_b[:mp, None, :], (mp, n_h, rdim)), zpad], axis=-1)
    k_cat = k_cat.reshape(b, tp, n_h * dq)
    o_p = _flash(_pad_seq(q_cat, lp), _pad_seq(k_cat, lp), _pad_seq(vv.reshape(b, tp, n_h * vdim), lp),
                 n_h, t, scale)[:, :tp].reshape(mp, n_h * vdim)

    q_lat = _headmm(q_nope[mp:].reshape(bs * ts, n_h * nope), w_uk.astype(BF16), BF16)
    q_lat = q_lat.reshape(bs, ts * n_h, clat)
    q_pe_s = q_pe[mp:].reshape(bs, ts * n_h, rdim)
    o_lat = _mla_sample(page_table, q_lat, q_pe_s, _pad_seq(ckv_b[mp:].reshape(bs, ts, clat), LANES),
                        _pad_seq(kpe_b[mp:].reshape(bs, ts, rdim), LANES), cache_ckv, jnp.swapaxes(cache_kpe, 2, 3),
                        layer, n_h, scale)
    o_s = _headmm(o_lat.reshape(bs * ts, n_h * clat), w_uv.astype(BF16), BF16)

    o = jnp.concatenate([o_p, o_s], axis=0)
    new = ((ckv[:mp].reshape(b, tp, clat), kpe[:mp].reshape(b, tp, rdim)),
           (ckv[mp:].reshape(bs, ts, clat), kpe[mp:].reshape(bs, ts, rdim)))
    return o, wo.astype(BF16), new


def _conv_layer(h, dims, state, w):
    w_in, taps, w_out = w
    b, tp, bs, ts = dims
    mp = b * tp
    d = w_out.shape[0]
    p = _matmul(h, w_in.astype(BF16))
    zeros = jnp.zeros((b, taps.shape[0] - 1, d), F32)
    y_p, st_p = _short_conv(p[:mp].reshape(b, tp, 3 * d), zeros, taps, 1)
    y_s, st_s = _short_conv(p[mp:].reshape(bs, ts, 3 * d), state, taps, _tile(bs, 128, 1))
    y = jnp.concatenate([y_p.reshape(mp, d), y_s.reshape(bs * ts, d)], axis=0)
    return y, w_out.astype(BF16), (st_p, st_s)


def kernel(x_prompt, x_sample, cache_a_k, cache_a_v, cache_a_idx_k, cache_mla_ckv, cache_mla_kpe, state_conv, page_table, meta_tokens, rel_bias, norm_mix, norm_ffn, a_wq, a_wk, a_wv, a_wo, a_q_norm, a_k_norm, a_w_iq, a_w_ik, a_w_iw, b_w_dq, b_q_a_norm, b_w_uq, b_q_norm, b_w_dkv, b_kv_norm, b_kpe_norm, b_w_uk, b_w_uv, b_wo, c_w_in, c_conv, c_w_out, ffn_wg, ffn_wu, ffn_wd):
    b, seq, d = x_prompt.shape
    bs, ts, _ = x_sample.shape
    n_meta = meta_tokens.shape[0]
    tp = seq + n_meta
    mp, ms = b * tp, bs * ts
    depth = norm_mix.shape[0]
    n_mixers = 3
    past = page_table.shape[1] * cache_a_k.shape[2]
    dims = (b, tp, bs, ts)

    meta = jnp.broadcast_to(meta_tokens[None].astype(x_prompt.dtype), (b, n_meta, d))
    x = jnp.concatenate([jnp.concatenate([meta, x_prompt], axis=1).reshape(mp, d),
                         x_sample.reshape(ms, d)], axis=0)

    rdim = b_kpe_norm.shape[1]
    half = rdim // 2
    pos = jnp.concatenate([jnp.tile(jnp.arange(tp, dtype=jnp.int32), b),
                           jnp.tile(past + jnp.arange(ts, dtype=jnp.int32), bs)])
    inv = jnp.power(ROPE_THETA, -jnp.arange(half, dtype=F32) / half)
    ang = pos.astype(F32)[:, None] * inv
    cos, sin = jnp.cos(ang), jnp.sin(ang)
    cos2 = jnp.concatenate([cos, cos], axis=-1)[:, None, :]
    sin2 = jnp.concatenate([-sin, sin], axis=-1)[:, None, :]

    a_new, b_new, c_new = [], [], []
    for i in range(depth):
        kind, j = i % n_mixers, i // n_mixers
        h = _rmsnorm(x, norm_mix[i], BF16)
        if kind == 0:
            w = (a_wq[j], a_wk[j], a_wv[j], a_wo[j], a_q_norm[j], a_k_norm[j], a_w_iq[j], a_w_ik[j], a_w_iw[j])
            o, wo, new = _dsa_layer(h, dims, page_table, rel_bias, cache_a_k, cache_a_v, cache_a_idx_k, j, w)
            a_new.append(new)
        elif kind == 1:
            w = (b_w_dq[j], b_q_a_norm[j], b_w_uq[j], b_q_norm[j], b_w_dkv[j], b_kv_norm[j], b_kpe_norm[j],
                 b_w_uk[j], b_w_uv[j], b_wo[j])
            o, wo, new = _mla_layer(h, dims, cos2, sin2, page_table, cache_mla_ckv, cache_mla_kpe, j, w)
            b_new.append(new)
        else:
            w = (c_w_in[j], c_conv[j], c_w_out[j])
            o, wo, new = _conv_layer(h, dims, state_conv[j], w)
            c_new.append(new)
        x = _matmul(o, wo, residual=x)
        hf = _rmsnorm(x, norm_ffn[i], BF16)
        x = _ffn(hf, x, ffn_wg[i].astype(BF16), ffn_wu[i].astype(BF16), ffn_wd[i].astype(BF16))

    y_prompt = x[:mp].reshape(b, tp, d)[:, n_meta:]
    y_sample = x[mp:].reshape(bs, ts, d)

    def stack(items, side, idx):
        return jnp.stack([it[side][idx] for it in items])

    return (y_prompt, y_sample,
            stack(a_new, 0, 0), stack(a_new, 0, 1), stack(a_new, 0, 2),
            stack(b_new, 0, 0), stack(b_new, 0, 1), jnp.stack([c[0] for c in c_new]),
            stack(a_new, 1, 0), stack(a_new, 1, 1), stack(a_new, 1, 2),
            stack(b_new, 1, 0), stack(b_new, 1, 1), jnp.stack([c[1] for c in c_new]))
```
